```python
import jax, jax.numpy as jnp
from jax import lax
import numpy as np

D_MODEL = 2048
BATCH = 8
SEQ = 2048
DEPTH = 1

MEM_LEN = 256
RMS_EPS = 1e-5
SSD_HEAD_DIM = 64
SSD_INNER = D_MODEL
SSD_HEADS = SSD_INNER // SSD_HEAD_DIM
SSD_GROUPS = 8
SSD_STATE = 128
SSD_CONV = 4
SSD_CHUNK = 128
SSD_CONV_DIM = SSD_INNER + 2 * SSD_GROUPS * SSD_STATE
FOX_HEAD_DIM = 128
FOX_INNER = D_MODEL
FOX_HEADS = FOX_INNER // FOX_HEAD_DIM
FOX_BLOCK = 128
MIX_WIDTH = SSD_INNER + FOX_INNER
IN_SPLITS = (SSD_INNER, SSD_CONV_DIM, SSD_HEADS, FOX_INNER, FOX_INNER, FOX_INNER, FOX_HEADS)
IN_PROJ_DIM = sum(IN_SPLITS)
XA_HEADS = 4
XA_HEAD_DIM = D_MODEL // XA_HEADS
N_EXPERTS = 32
TOP_K = 4
D_EXPERT = D_MODEL
SWIGLU_LIMIT = 7.0
SWIGLU_ALPHA = 1.702
MOE_BLOCK = 128

kernel_name = "hybrid_ssd_fox_memxattn_moe"


def rmsnorm(x, g):
    xf = x.astype(jnp.float32)
    y = xf * lax.rsqrt(jnp.mean(xf * xf, axis=-1, keepdims=True) + RMS_EPS)
    return (y * g.astype(jnp.float32)).astype(x.dtype)


def causal_dwconv(u, w, b):
    k = w.shape[0]
    out = lax.conv_general_dilated(u, w[:, None, :], window_strides=(1,), padding=[(k - 1, 0)],
                                   dimension_numbers=('NWC', 'WIO', 'NWC'),
                                   feature_group_count=u.shape[-1])
    return out + b


def ssd_scan(x, dt, A, Bm, Cm, D):
    b, l, h, p = x.shape
    g, n = Bm.shape[2], Bm.shape[3]
    r = h // g
    c = l // SSD_CHUNK
    L = SSD_CHUNK
    xc = x.reshape(b, c, L, g, r, p)
    dtc = dt.reshape(b, c, L, g, r)
    Bc = Bm.reshape(b, c, L, g, n)
    Cc = Cm.reshape(b, c, L, g, n)
    cs = jnp.cumsum(dtc * A.reshape(g, r), axis=2)
    xdt = xc * dtc[..., None].astype(x.dtype)
    causal = jnp.tril(jnp.ones((L, L), dtype=bool))[None, None, :, :, None, None]
    seg = cs[:, :, :, None] - cs[:, :, None, :]
    decay_ls = jnp.exp(jnp.where(causal, seg, -jnp.inf))
    cb = jnp.einsum('bclgn,bcsgn->bclsg', Cc, Bc)
    y_diag = jnp.einsum('bclsgr,bcsgrp->bclgrp', cb[..., None] * decay_ls, xdt)
    decay_end = jnp.exp(cs[:, :, -1:] - cs)
    states = jnp.einsum('bclgn,bclgr,bclgrp->bcgrpn', Bc, decay_end, xdt)
    chunk_decay = jnp.exp(cs[:, :, -1])

    def step(carry, inp):
        s, a = inp
        return carry * a[..., None, None] + s, carry

    h0 = jnp.zeros_like(states[:, 0])
    _, prev = lax.scan(step, h0, (jnp.moveaxis(states, 1, 0), jnp.moveaxis(chunk_decay, 1, 0)))
    prev = jnp.moveaxis(prev, 0, 1)
    y_off = jnp.einsum('bclgn,bcgrpn,bclgr->bclgrp', Cc, prev, jnp.exp(cs))
    y = y_diag + y_off + xc * D.reshape(g, r)[..., None]
    return y.reshape(b, l, h * p).astype(x.dtype)


def forgetting_attention(q, k, v, logf):
    b, l, H, d = q.shape
    scale = d ** -0.5
    F = jnp.moveaxis(jnp.cumsum(logf, axis=1), 1, 2)
    qh, kh, vh = (jnp.moveaxis(t, 1, 2) for t in (q, k, v))
    outs = []
    for i in range(l // FOX_BLOCK):
        q0, q1 = i * FOX_BLOCK, (i + 1) * FOX_BLOCK
        s = jnp.einsum('bhqd,bhkd->bhqk', qh[:, :, q0:q1], kh[:, :, :q1]).astype(jnp.float32) * scale
        s = s + (F[:, :, q0:q1, None] - F[:, :, None, :q1])
        mask = jnp.arange(q1)[None, :] <= (q0 + jnp.arange(FOX_BLOCK))[:, None]
        p = jax.nn.softmax(jnp.where(mask, s, -jnp.inf), axis=-1)
        outs.append(jnp.einsum('bhqk,bhkd->bhqd', p.astype(vh.dtype), vh[:, :, :q1]))
    o = jnp.concatenate(outs, axis=2)
    return jnp.moveaxis(o, 1, 2).reshape(b, l, H * d)


def memory_cross_attention(u, m, w_q, w_kv, w_o):
    b, l, _ = u.shape
    q = (u @ w_q).reshape(b, l, XA_HEADS, XA_HEAD_DIM)
    kv = m @ w_kv
    k = kv[..., :D_MODEL].reshape(b, m.shape[1], XA_HEADS, XA_HEAD_DIM)
    v = kv[..., D_MODEL:].reshape(b, m.shape[1], XA_HEADS, XA_HEAD_DIM)
    s = jnp.einsum('bqhd,bkhd->bhqk', q, k).astype(jnp.float32) * (XA_HEAD_DIM ** -0.5)
    p = jax.nn.softmax(s, axis=-1).astype(v.dtype)
    o = jnp.einsum('bhqk,bkhd->bqhd', p, v).reshape(b, l, D_MODEL)
    return o @ w_o


def moe_ffn(x, w_router, b_router, w1, b1, w2, b2):
    N, D = x.shape
    logits = (x @ w_router).astype(jnp.float32) + b_router.astype(jnp.float32)
    top_val, top_idx = lax.top_k(logits, TOP_K)
    gate = jax.nn.softmax(top_val, axis=-1)
    e_flat = top_idx.reshape(-1)
    onehot = jax.nn.one_hot(e_flat, N_EXPERTS, dtype=jnp.int32)
    counts = jnp.sum(onehot, axis=0)
    rank = jnp.take_along_axis(jnp.cumsum(onehot, axis=0), e_flat[:, None], axis=1)[:, 0] - 1
    padded = ((counts + MOE_BLOCK - 1) // MOE_BLOCK) * MOE_BLOCK
    pad_end = jnp.cumsum(padded)
    pad_start = pad_end - padded
    dest = pad_start[e_flat] + rank
    n_blocks = -(-(N * TOP_K) // MOE_BLOCK) + N_EXPERTS
    buf = jnp.zeros((n_blocks * MOE_BLOCK, D), x.dtype).at[dest].set(jnp.repeat(x, TOP_K, axis=0))
    block_expert = jnp.minimum(jnp.searchsorted(pad_end, jnp.arange(n_blocks) * MOE_BLOCK, side='right'),
                               N_EXPERTS - 1)

    def expert_block(args):
        xb, e = args
        gu = xb @ w1[e] + b1[e]
        g, u = gu[:, :D_EXPERT], gu[:, D_EXPERT:]
        g = jnp.minimum(g, SWIGLU_LIMIT)
        u = jnp.clip(u, -SWIGLU_LIMIT, SWIGLU_LIMIT)
        act = g * jax.nn.sigmoid(SWIGLU_ALPHA * g) * (u + 1)
        return act @ w2[e] + b2[e]

    y_buf = lax.map(expert_block, (buf.reshape(n_blocks, MOE_BLOCK, D), block_expert))
    y = y_buf.reshape(n_blocks * MOE_BLOCK, D)[dest].reshape(N, TOP_K, D)
    return jnp.einsum('nkd,nk->nd', y, gate.astype(y.dtype))


def setup_inputs(seed: int = 0) -> dict:
    key = jax.random.key(seed)
    ks = jax.random.split(key, 32)
    f32 = jnp.float32
    nrm = lambda k, shape, s: jax.random.normal(k, shape, f32) * s
    gain = lambda k, shape: 1.0 + 0.05 * jax.random.normal(k, shape, f32)
    dt0 = jnp.exp(jax.random.uniform(ks[5], (DEPTH, SSD_HEADS), f32, np.log(1e-3), np.log(1e-1)))
    return {
        "x": jax.random.normal(ks[0], (BATCH, SEQ, D_MODEL), f32),
        "mem": jax.random.normal(ks[1], (BATCH, MEM_LEN, D_MODEL), f32),
        "ln_mix": gain(ks[2], (DEPTH, D_MODEL)),
        "w_in": nrm(ks[3], (DEPTH, D_MODEL, IN_PROJ_DIM), D_MODEL ** -0.5),
        "conv_w": nrm(ks[4], (DEPTH, SSD_CONV, SSD_CONV_DIM), SSD_CONV ** -0.5),
        "conv_b": nrm(ks[6], (DEPTH, SSD_CONV_DIM), 0.01),
        "dt_bias": dt0 + jnp.log(-jnp.expm1(-dt0)),
        "a_log": jnp.log(jax.random.uniform(ks[7], (DEPTH, SSD_HEADS), f32, 1.0, 16.0)),
        "d_skip": gain(ks[8], (DEPTH, SSD_HEADS)),
        "ssd_norm": gain(ks[9], (DEPTH, SSD_INNER)),
        "fgate_bias": jax.random.uniform(ks[10], (DEPTH, FOX_HEADS), f32, 2.0, 5.0),
        "w_out": nrm(ks[11], (DEPTH, MIX_WIDTH, D_MODEL), MIX_WIDTH ** -0.5),
        "ln_xa": gain(ks[12], (DEPTH, D_MODEL)),
        "ln_mem": gain(ks[13], (DEPTH, D_MODEL)),
        "w_xq": nrm(ks[14], (DEPTH, D_MODEL, D_MODEL), D_MODEL ** -0.5),
        "w_xkv": nrm(ks[15], (DEPTH, D_MODEL, 2 * D_MODEL), D_MODEL ** -0.5),
        "w_xo": nrm(ks[16], (DEPTH, D_MODEL, D_MODEL), D_MODEL ** -0.5),
        "ln_ffn": gain(ks[17], (DEPTH, D_MODEL)),
        "w_router": nrm(ks[18], (DEPTH, D_MODEL, N_EXPERTS), D_MODEL ** -0.5),
        "b_router": nrm(ks[19], (DEPTH, N_EXPERTS), 0.01),
        "w_moe1": nrm(ks[20], (DEPTH, N_EXPERTS, D_MODEL, 2 * D_EXPERT), D_MODEL ** -0.5),
        "b_moe1": nrm(ks[21], (DEPTH, N_EXPERTS, 2 * D_EXPERT), 0.01),
        "w_moe2": nrm(ks[22], (DEPTH, N_EXPERTS, D_EXPERT, D_MODEL), D_EXPERT ** -0.5),
        "b_moe2": nrm(ks[23], (DEPTH, N_EXPERTS, D_MODEL), 0.01),
        "ln_final": gain(ks[24], (D_MODEL,)),
    }


def reference(x, mem, ln_mix, w_in, conv_w, conv_b, dt_bias, a_log, d_skip, ssd_norm, fgate_bias,
              w_out, ln_xa, ln_mem, w_xq, w_xkv, w_xo, ln_ffn, w_router, b_router,
              w_moe1, b_moe1, w_moe2, b_moe2, ln_final):
    b, l, _ = x.shape
    split_idx = list(np.cumsum(IN_SPLITS)[:-1])
    h = x
    for i in range(DEPTH):
        u = rmsnorm(h, ln_mix[i])
        z, xbc, dt_raw, q, k, v, f_raw = jnp.split(u @ w_in[i], split_idx, axis=-1)
        xbc = jax.nn.silu(causal_dwconv(xbc, conv_w[i], conv_b[i]))
        xs, Bm, Cm = jnp.split(xbc, [SSD_INNER, SSD_INNER + SSD_GROUPS * SSD_STATE], axis=-1)
        dt = jax.nn.softplus(dt_raw.astype(jnp.float32) + dt_bias[i].astype(jnp.float32))
        A = -jnp.exp(a_log[i].astype(jnp.float32))
        y_ssd = ssd_scan(xs.reshape(b, l, SSD_HEADS, SSD_HEAD_DIM), dt, A,
                         Bm.reshape(b, l, SSD_GROUPS, SSD_STATE), Cm.reshape(b, l, SSD_GROUPS, SSD_STATE),
                         d_skip[i])
        y_ssd = rmsnorm(y_ssd * jax.nn.silu(z), ssd_norm[i])
        logf = jax.nn.log_sigmoid(f_raw.astype(jnp.float32) + fgate_bias[i].astype(jnp.float32))
        y_fox = forgetting_attention(q.reshape(b, l, FOX_HEADS, FOX_HEAD_DIM),
                                     k.reshape(b, l, FOX_HEADS, FOX_HEAD_DIM),
                                     v.reshape(b, l, FOX_HEADS, FOX_HEAD_DIM), logf)
        h = h + jnp.concatenate([y_ssd, y_fox], axis=-1) @ w_out[i]
        h = h + memory_cross_attention(rmsnorm(h, ln_xa[i]), rmsnorm(mem, ln_mem[i]),
                                       w_xq[i], w_xkv[i], w_xo[i])
        y = moe_ffn(rmsnorm(h, ln_ffn[i]).reshape(b * l, D_MODEL), w_router[i], b_router[i],
                    w_moe1[i], b_moe1[i], w_moe2[i], b_moe2[i])
        h = h + y.reshape(b, l, D_MODEL)
    return rmsnorm(h, ln_final)
```

```python
import functools

import jax
import jax.numpy as jnp
from jax import lax
from jax.experimental import pallas as pl
from jax.experimental.pallas import tpu as pltpu

F32 = jnp.float32
BF16 = jnp.bfloat16
I32 = jnp.int32

RMS_EPS = 1e-5
LANES = 128
SSD_HEAD_DIM = 64
SSD_GROUPS = 8
SSD_STATE = 128
SSD_CONV = 4
SSD_CHUNK = 128
CONV_HALO = 8
FOX_HEAD_DIM = 128
XA_HEADS = 4
TOP_K = 4
SWIGLU_LIMIT = 7.0
SWIGLU_ALPHA = 1.702
NEG_BIG = -1e30
VMEM_LIMIT = 56 * 1024 * 1024

MOE_TM = 512
MOE_TC = 512


def _cparams(sem):
    return pltpu.CompilerParams(dimension_semantics=sem, vmem_limit_bytes=VMEM_LIMIT)


def _rms(x, g):
    ms = jnp.mean(x * x, axis=-1, keepdims=True)
    return x * lax.rsqrt(ms + RMS_EPS) * g


def _softplus(x):
    return jnp.maximum(x, 0.0) + jnp.log1p(jnp.exp(-jnp.abs(x)))


def _split3(x):
    hi = x.astype(BF16)
    r1 = x - hi.astype(F32)
    mid = r1.astype(BF16)
    lo = (r1 - mid.astype(F32)).astype(BF16)
    return hi, mid, lo


def _dot01_right(x, m01):
    hi, mid, lo = _split3(x)
    d = lambda a: jnp.dot(a, m01, preferred_element_type=F32)
    return d(hi) + d(mid) + d(lo)


def _dot01_left(m01, x):
    hi, mid, lo = _split3(x)
    d = lambda a: jnp.dot(m01, a, preferred_element_type=F32)
    return d(hi) + d(mid) + d(lo)


def _inproj_kernel(x_ref, g_ref, w_ref, ws_ref, o_ref, os_ref, u_scr):
    @pl.when(pl.program_id(1) == 0)
    def _():
        u = _rms(x_ref[...], g_ref[...]).astype(BF16)
        u_scr[...] = u
        os_ref[...] = jnp.dot(u, ws_ref[...], preferred_element_type=F32)

    o_ref[...] = jnp.dot(u_scr[...], w_ref[...], preferred_element_type=F32).astype(o_ref.dtype)


def _inproj(x, g, w_main, w_small, tm, tn):
    m, k = x.shape
    n = w_main.shape[1]
    return pl.pallas_call(
        _inproj_kernel,
        grid=(m // tm, n // tn),
        in_specs=[
            pl.BlockSpec((tm, k), lambda i, j: (i, 0)),
            pl.BlockSpec((1, k), lambda i, j: (0, 0)),
            pl.BlockSpec((k, tn), lambda i, j: (0, j)),
            pl.BlockSpec((k, LANES), lambda i, j: (0, 0)),
        ],
        out_specs=[
            pl.BlockSpec((tm, tn), lambda i, j: (i, j)),
            pl.BlockSpec((tm, LANES), lambda i, j: (i, 0)),
        ],
        out_shape=[jax.ShapeDtypeStruct((m, n), BF16), jax.ShapeDtypeStruct((m, LANES), F32)],
        scratch_shapes=[pltpu.VMEM((tm, k), BF16)],
        compiler_params=_cparams(("parallel", "arbitrary")),
        name="inproj",
    )(x, g, w_main, w_small)


def _norm_matmul_kernel(x_ref, g_ref, w_ref, o_ref, u_scr):
    @pl.when(pl.program_id(1) == 0)
    def _():
        u_scr[...] = _rms(x_ref[...], g_ref[...]).astype(BF16)

    o_ref[...] = jnp.dot(u_scr[...], w_ref[...], preferred_element_type=F32).astype(o_ref.dtype)


def _norm_matmul(x, g, w, tm, tn, name):
    m, k = x.shape
    n = w.shape[1]
    return pl.pallas_call(
        _norm_matmul_kernel,
        grid=(m // tm, n // tn),
        in_specs=[
            pl.BlockSpec((tm, k), lambda i, j: (i, 0)),
            pl.BlockSpec((1, k), lambda i, j: (0, 0)),
            pl.BlockSpec((k, tn), lambda i, j: (0, j)),
        ],
        out_specs=pl.BlockSpec((tm, tn), lambda i, j: (i, j)),
        out_shape=jax.ShapeDtypeStruct((m, n), BF16),
        scratch_shapes=[pltpu.VMEM((tm, k), BF16)],
        compiler_params=_cparams(("parallel", "arbitrary")),
        name=name,
    )(x, g, w)


def _mm_res_kernel(*refs, n_lhs):
    res_ref, o_ref = refs[2 * n_lhs], refs[2 * n_lhs + 1]
    acc = res_ref[...]
    for a_ref, w_ref in zip(refs[:n_lhs], refs[n_lhs:2 * n_lhs]):
        acc = acc + jnp.dot(a_ref[...], w_ref[...], preferred_element_type=F32)
    o_ref[...] = acc


def _mm_res(lhs, ws, res, tm, tn, name):
    m, n = res.shape
    n_lhs = len(lhs)
    in_specs = [pl.BlockSpec((tm, a.shape[1]), lambda i, j: (i, 0)) for a in lhs]
    in_specs += [pl.BlockSpec((w.shape[0], tn), lambda i, j: (0, j)) for w in ws]
    in_specs += [pl.BlockSpec((tm, tn), lambda i, j: (i, j))]
    return pl.pallas_call(
        functools.partial(_mm_res_kernel, n_lhs=n_lhs),
        grid=(m // tm, n // tn),
        in_specs=in_specs,
        out_specs=pl.BlockSpec((tm, tn), lambda i, j: (i, j)),
        out_shape=jax.ShapeDtypeStruct((m, n), F32),
        compiler_params=_cparams(("parallel", "arbitrary")),
        name=name,
    )(*lhs, *ws, res)


def _ssd_kernel(xbc_ref, z_ref, sm_ref, cw_ref, cb_ref, dtb_ref, alog_ref, fb_ref, dch_ref, gn_ref, e_ref,
                y_ref, fcol_ref, frow_ref,
                ext, cv, state, fcarry, yacc, dtch, csch, *, d_inner, n_heads, fox_lane0):
    L = SSD_CHUNK
    gw = d_inner // SSD_GROUPS
    hpg = gw // SSD_HEAD_DIM
    cdim = d_inner + 2 * SSD_GROUPS * SSD_STATE
    strip = 256

    @pl.when(pl.program_id(1) == 0)
    def _():
        ext[0:CONV_HALO, :] = jnp.zeros((CONV_HALO, cdim), F32)
        state[...] = jnp.zeros_like(state)
        fcarry[...] = jnp.zeros_like(fcarry)

    ext[CONV_HALO:CONV_HALO + L, :] = xbc_ref[...].astype(F32)
    for s0 in range(0, cdim, strip):
        acc = jnp.broadcast_to(cb_ref[:, s0:s0 + strip], (L, strip))
        for k in range(SSD_CONV):
            r0 = CONV_HALO - (SSD_CONV - 1) + k
            acc = acc + cw_ref[k:k + 1, s0:s0 + strip] * ext[r0:r0 + L, s0:s0 + strip]
        cv[:, s0:s0 + strip] = acc * jax.nn.sigmoid(acc)
    ext[0:CONV_HALO, :] = ext[L:L + CONV_HALO, :]

    row = lax.broadcasted_iota(I32, (L, L), 0)
    col = lax.broadcasted_iota(I32, (L, L), 1)
    causal = col <= row
    tri = causal.astype(BF16)

    sm = sm_ref[...]
    dt = _softplus(sm + dtb_ref[...])
    dta = dt * (-jnp.exp(alog_ref[...]))
    cs = _dot01_left(tri, dta)
    cst = cs.T
    logf = -_softplus(-(sm + fb_ref[...]))
    fc = _dot01_left(tri, logf) + fcarry[...]
    fcol_ref[...] = fc
    frow_ref[...] = fc.T
    fcarry[...] = fc[L - 1:L, :]

    e01 = e_ref[...]
    dtch[...] = _dot01_right(dt, e01)
    csch[...] = _dot01_right(cs, e01)

    lane_g = lax.broadcasted_iota(I32, (1, gw), 1)
    for g in range(SSD_GROUPS):
        c0 = g * gw
        bg = cv[:, d_inner + g * SSD_STATE:d_inner + (g + 1) * SSD_STATE]
        cg = cv[:, d_inner + SSD_GROUPS * SSD_STATE + g * SSD_STATE:d_inner + SSD_GROUPS * SSD_STATE + (g + 1) * SSD_STATE]
        cgb = cg.astype(BF16)
        cbm = lax.dot_general(cgb, bg.astype(BF16), (((1,), (1,)), ((), ())), preferred_element_type=F32)
        bgt = bg.T.astype(BF16)
        xg = cv[:, c0:c0 + gw]
        csg = csch[:, c0:c0 + gw]
        cs_last = csg[L - 1:L, :]
        xdt = xg * dtch[:, c0:c0 + gw]
        xw = (xdt * jnp.exp(cs_last - csg)).astype(BF16)
        st_new = jnp.dot(bgt, xw, preferred_element_type=F32)
        prev = state[g]
        yg = jnp.dot(cgb, prev.astype(BF16), preferred_element_type=F32) * jnp.exp(csg)
        state[g] = prev * jnp.exp(cs_last) + st_new
        yg = yg + xg * dch_ref[:, c0:c0 + gw]
        for r in range(hpg):
            h = g * hpg + r
            seg = cs[:, h:h + 1] - cst[h:h + 1, :]
            m = (cbm * jnp.exp(jnp.where(causal, seg, NEG_BIG))).astype(BF16)
            head = (lane_g >= r * SSD_HEAD_DIM) & (lane_g < (r + 1) * SSD_HEAD_DIM)
            xh = jnp.where(head, xdt, 0.0).astype(BF16)
            yg = yg + jnp.dot(m, xh, preferred_element_type=F32)
        yacc[:, c0:c0 + gw] = yg

    z = z_ref[...].astype(F32)
    gated = yacc[...] * (z * jax.nn.sigmoid(z))
    y_ref[...] = _rms(gated, gn_ref[...]).astype(y_ref.dtype)


def _ssd(pmain, psmall, conv_w, conv_b, dtb, alog, fb, dch, gnorm, e01, *, batch, seq, d_inner, n_heads, fox_lane0):
    nc = seq // SSD_CHUNK
    cdim = d_inner + 2 * SSD_GROUPS * SSD_STATE
    gw = d_inner // SSD_GROUPS
    m = batch * seq
    rowblk = lambda b, c: b * nc + c
    const = lambda shape: pl.BlockSpec(shape, lambda b, c: (0, 0))
    kern = functools.partial(_ssd_kernel, d_inner=d_inner, n_heads=n_heads, fox_lane0=fox_lane0)
    return pl.pallas_call(
        kern,
        grid=(batch, nc),
        in_specs=[
            pl.BlockSpec((SSD_CHUNK, cdim), lambda b, c: (rowblk(b, c), 0)),
            pl.BlockSpec((SSD_CHUNK, d_inner), lambda b, c: (rowblk(b, c), cdim // d_inner)),
            pl.BlockSpec((SSD_CHUNK, LANES), lambda b, c: (rowblk(b, c), 0)),
            const((SSD_CONV, cdim)), const((1, cdim)), const((1, LANES)), const((1, LANES)), const((1, LANES)),
            const((1, d_inner)), const((1, d_inner)), const((LANES, d_inner)),
        ],
        out_specs=[
            pl.BlockSpec((SSD_CHUNK, d_inner), lambda b, c: (rowblk(b, c), 0)),
            pl.BlockSpec((SSD_CHUNK, LANES), lambda b, c: (rowblk(b, c), 0)),
            pl.BlockSpec((None, LANES, SSD_CHUNK), lambda b, c: (b, 0, c)),
        ],
        out_shape=[
            jax.ShapeDtypeStruct((m, d_inner), BF16),
            jax.ShapeDtypeStruct((m, LANES), F32),
            jax.ShapeDtypeStruct((batch, LANES, seq), F32),
        ],
        scratch_shapes=[
            pltpu.VMEM((CONV_HALO + SSD_CHUNK, cdim), F32),
            pltpu.VMEM((SSD_CHUNK, cdim), F32),
            pltpu.VMEM((SSD_GROUPS, SSD_STATE, gw), F32),
            pltpu.VMEM((1, LANES), F32),
            pltpu.VMEM((SSD_CHUNK, d_inner), F32),
            pltpu.VMEM((SSD_CHUNK, d_inner), F32),
            pltpu.VMEM((SSD_CHUNK, d_inner), F32),
        ],
        compiler_params=_cparams(("arbitrary", "arbitrary")),
        name="ssd_scan",
    )(pmain, pmain, psmall, conv_w, conv_b, dtb, alog, fb, dch, gnorm, e01)


def _fox_kernel(q_ref, k_ref, v_ref, fcol_ref, frow_ref, o_ref, *, tq, fox_lane0):
    h = pl.program_id(1)
    qi = pl.program_id(2)
    scale = FOX_HEAD_DIM ** -0.5
    q = (q_ref[...].astype(F32) * scale).astype(BF16)
    lane = lax.broadcasted_iota(I32, (tq, LANES), 1)
    fq = jnp.sum(jnp.where(lane == fox_lane0 + h, fcol_ref[...], 0.0), axis=-1, keepdims=True)

    def scores(kb):
        ks = pl.multiple_of(kb * tq, tq)
        k = k_ref[pl.ds(ks, tq), :]
        fk = frow_ref[pl.ds(fox_lane0 + h, 1), pl.ds(ks, tq)]
        s = lax.dot_general(q, k, (((1,), (1,)), ((), ())), preferred_element_type=F32)
        return s + (fq - fk), ks

    def update(s, ks, carry):
        m, l, acc = carry
        m_new = jnp.maximum(m, jnp.max(s, axis=-1, keepdims=True))
        alpha = jnp.exp(m - m_new)
        p = jnp.exp(s - m_new)
        l = alpha * l + jnp.sum(p, axis=-1, keepdims=True)
        acc = alpha * acc + jnp.dot(p.astype(BF16), v_ref[pl.ds(ks, tq), :], preferred_element_type=F32)
        return m_new, l, acc

    def body(kb, carry):
        s, ks = scores(kb)
        return update(s, ks, carry)

    init = (jnp.full((tq, 1), NEG_BIG, F32), jnp.zeros((tq, 1), F32), jnp.zeros((tq, FOX_HEAD_DIM), F32))
    carry = lax.fori_loop(0, qi, body, init)
    s, ks = scores(qi)
    row = lax.broadcasted_iota(I32, (tq, tq), 0)
    col = lax.broadcasted_iota(I32, (tq, tq), 1)
    s = jnp.where(col <= row, s, NEG_BIG)
    _, l, acc = update(s, ks, carry)
    o_ref[...] = (acc * (1.0 / l)).astype(o_ref.dtype)


def _fox(pmain, fcol, frow, *, batch, seq, n_heads, q_col, k_col, v_col, fox_lane0, tq):
    nq = seq // tq
    m = batch * seq
    kern = functools.partial(_fox_kernel, tq=tq, fox_lane0=fox_lane0)
    return pl.pallas_call(
        kern,
        grid=(batch, n_heads, nq),
        in_specs=[
            pl.BlockSpec((tq, FOX_HEAD_DIM), lambda b, h, i: (b * nq + i, q_col + h)),
            pl.BlockSpec((seq, FOX_HEAD_DIM), lambda b, h, i: (b, k_col + h)),
            pl.BlockSpec((seq, FOX_HEAD_DIM), lambda b, h, i: (b, v_col + h)),
            pl.BlockSpec((tq, LANES), lambda b, h, i: (b * nq + i, 0)),
            pl.BlockSpec((None, LANES, seq), lambda b, h, i: (b, 0, 0)),
        ],
        out_specs=pl.BlockSpec((tq, FOX_HEAD_DIM), lambda b, h, i: (b * nq + i, h)),
        out_shape=jax.ShapeDtypeStruct((m, n_heads * FOX_HEAD_DIM), BF16),
        compiler_params=_cparams(("parallel", "parallel", "arbitrary")),
        name="fox_attn",
    )(pmain, pmain, pmain, fcol, frow)


def _xattn_kernel(q_ref, kv_ref, o_ref, *, d_model):
    hd = d_model // XA_HEADS
    scale = hd ** -0.5
    for h in range(XA_HEADS):
        q = q_ref[:, h * hd:(h + 1) * hd]
        k = kv_ref[:, h * hd:(h + 1) * hd]
        v = kv_ref[:, d_model + h * hd:d_model + (h + 1) * hd]
        s = lax.dot_general(q, k, (((1,), (1,)), ((), ())), preferred_element_type=F32) * scale
        p = jnp.exp(s - jnp.max(s, axis=-1, keepdims=True))
        p = p * (1.0 / jnp.sum(p, axis=-1, keepdims=True))
        o_ref[:, h * hd:(h + 1) * hd] = jnp.dot(p.astype(BF16), v, preferred_element_type=F32).astype(o_ref.dtype)


def _xattn(q, kv, *, batch, seq, mem_len, d_model, tq):
    nq = seq // tq
    return pl.pallas_call(
        functools.partial(_xattn_kernel, d_model=d_model),
        grid=(batch, nq),
        in_specs=[
            pl.BlockSpec((tq, d_model), lambda b, i: (b * nq + i, 0)),
            pl.BlockSpec((mem_len, 2 * d_model), lambda b, i: (b, 0)),
        ],
        out_specs=pl.BlockSpec((tq, d_model), lambda b, i: (b * nq + i, 0)),
        out_shape=jax.ShapeDtypeStruct((batch * seq, d_model), BF16),
        compiler_params=_cparams(("parallel", "arbitrary")),
        name="mem_xattn",
    )(q, kv)


def _router_kernel(h_ref, g_ref, wr_ref, br_ref, u_ref, idx_ref, gate_ref, cnt_ref, *, n_experts):
    tm = h_ref.shape[0]

    @pl.when(pl.program_id(0) == 0)
    def _():
        cnt_ref[...] = jnp.zeros_like(cnt_ref)

    u = _rms(h_ref[...], g_ref[...])
    u_ref[...] = u
    logits = jnp.dot(u, wr_ref[...], preferred_element_type=F32, precision=lax.Precision.HIGHEST) + br_ref[...]
    lane = lax.broadcasted_iota(I32, (tm, LANES), 1)
    work = jnp.where(lane < n_experts, logits, -jnp.inf)
    vals, idxs = [], []
    for _ in range(TOP_K):
        mx = jnp.max(work, axis=-1, keepdims=True)
        am = jnp.min(jnp.where(work == mx, lane, LANES), axis=-1, keepdims=True)
        vals.append(mx)
        idxs.append(am)
        work = jnp.where(lane == am, -jnp.inf, work)
    ex = [jnp.exp(v - vals[0]) for v in vals]
    den = ex[0] + ex[1] + ex[2] + ex[3]
    idx_out = jnp.zeros((tm, LANES), I32)
    gate_out = jnp.zeros((tm, LANES), F32)
    hits = jnp.zeros((tm, LANES), F32)
    for k in range(TOP_K):
        idx_out = jnp.where(lane == k, idxs[k], idx_out)
        gate_out = jnp.where(lane == k, ex[k] / den, gate_out)
        hits = hits + (lane == idxs[k]).astype(F32)
    idx_ref[...] = idx_out
    gate_ref[...] = gate_out
    cnt_ref[...] += jnp.broadcast_to(jnp.sum(hits, axis=0, keepdims=True), cnt_ref.shape)


def _router(h, g, wr, br, *, n_experts, tm):
    n, d = h.shape
    return pl.pallas_call(
        functools.partial(_router_kernel, n_experts=n_experts),
        grid=(n // tm,),
        in_specs=[
            pl.BlockSpec((tm, d), lambda i: (i, 0)),
            pl.BlockSpec((1, d), lambda i: (0, 0)),
            pl.BlockSpec((d, LANES), lambda i: (0, 0)),
            pl.BlockSpec((1, LANES), lambda i: (0, 0)),
        ],
        out_specs=[
            pl.BlockSpec((tm, d), lambda i: (i, 0)),
            pl.BlockSpec((tm, LANES), lambda i: (i, 0)),
            pl.BlockSpec((tm, LANES), lambda i: (i, 0)),
            pl.BlockSpec((8, LANES), lambda i: (0, 0)),
        ],
        out_shape=[
            jax.ShapeDtypeStruct((n, d), F32),
            jax.ShapeDtypeStruct((n, LANES), I32),
            jax.ShapeDtypeStruct((n, LANES), F32),
            jax.ShapeDtypeStruct((8, LANES), F32),
        ],
        compiler_params=_cparams(("arbitrary",)),
        name="moe_router",
    )(h, g, wr, br)


def _slots_kernel(idx_ref, cnt_ref, dest_ref, be_ref, meta_ref, carry, *, n_experts, tm_e, nb_pad):
    t = idx_ref.shape[0]
    lane1 = lax.broadcasted_iota(I32, (1, LANES), 1)

    @pl.when(pl.program_id(0) == 0)
    def _():
        cnt = cnt_ref[0:1, :]
        padded = jnp.floor((cnt + (tm_e - 1)) / tm_e) * tm_e
        er = lax.broadcasted_iota(I32, (LANES, LANES), 0)
        ec = lax.broadcasted_iota(I32, (LANES, LANES), 1)
        start = _dot01_right(jnp.broadcast_to(padded, (8, LANES)), (er < ec).astype(BF16))[0:1, :]
        carry[...] = start
        pend = start + padded
        jrow = (lax.broadcasted_iota(I32, (nb_pad, LANES), 0) * tm_e).astype(F32)
        lane = lax.broadcasted_iota(I32, (nb_pad, LANES), 1)
        hit = jnp.where((pend <= jrow) & (lane < n_experts), 1.0, 0.0)
        be = jnp.minimum(jnp.sum(hit, axis=-1, keepdims=True), n_experts - 1.0)
        be_ref[...] = jnp.broadcast_to(be, (nb_pad, LANES)).astype(I32)
        total = jnp.sum(jnp.where(lane1 < n_experts, padded, 0.0), axis=-1, keepdims=True)
        meta = jnp.concatenate([start, cnt, pend, jnp.broadcast_to(total / tm_e, (1, LANES)),
                                jnp.zeros((4, LANES), F32)], axis=0)
        meta_ref[...] = meta.astype(I32)

    idx = idx_ref[...]
    lane = lax.broadcasted_iota(I32, (t, LANES), 1)
    ohs = [lane == idx[:, k:k + 1] for k in range(TOP_K)]
    oh = jnp.zeros((t, LANES), F32)
    for o in ohs:
        oh = oh + o.astype(F32)
    r = lax.broadcasted_iota(I32, (t, t), 0)
    c = lax.broadcasted_iota(I32, (t, t), 1)
    earlier = jnp.dot((c < r).astype(BF16), oh.astype(BF16), preferred_element_type=F32)
    base = carry[...] + earlier
    dest = jnp.zeros((t, LANES), F32)
    for k in range(TOP_K):
        dk = jnp.sum(jnp.where(ohs[k], base, 0.0), axis=-1, keepdims=True)
        dest = jnp.where(lane == k, dk, dest)
    dest_ref[...] = dest.astype(I32)
    carry[...] += jnp.sum(oh, axis=0, keepdims=True)


def _slots(idx, cnt, *, n_experts, tm_e, nb_pad, t):
    n = idx.shape[0]
    return pl.pallas_call(
        functools.partial(_slots_kernel, n_experts=n_experts, tm_e=tm_e, nb_pad=nb_pad),
        grid=(n // t,),
        in_specs=[pl.BlockSpec((t, LANES), lambda i: (i, 0)), pl.BlockSpec((8, LANES), lambda i: (0, 0))],
        out_specs=[
            pl.BlockSpec((t, LANES), lambda i: (i, 0)),
            pl.BlockSpec((nb_pad, LANES), lambda i: (0, 0)),
            pl.BlockSpec((8, LANES), lambda i: (0, 0)),
        ],
        out_shape=[
            jax.ShapeDtypeStruct((n, LANES), I32),
            jax.ShapeDtypeStruct((nb_pad, LANES), I32),
            jax.ShapeDtypeStruct((8, LANES), I32),
        ],
        scratch_shapes=[pltpu.VMEM((1, LANES), F32)],
        compiler_params=_cparams(("arbitrary",)),
        name="moe_slots",
    )(idx, cnt)


def _dispatch_kernel(dest_sm, fill_lo_sm, fill_hi_sm, u_hbm, buf_hbm, zrow, sem, zsem, *, t, n_experts):
    i = pl.program_id(0)

    def row_copy(tok, slot):
        return pltpu.make_async_copy(u_hbm.at[pl.ds(tok, 1)], buf_hbm.at[pl.ds(slot, 1)], sem)

    def zero_copy(slot):
        return pltpu.make_async_copy(zrow.at[pl.ds(0, 1)], buf_hbm.at[pl.ds(slot, 1)], zsem)

    @pl.when(i == 0)
    def _():
        zrow[...] = jnp.zeros_like(zrow)
        for e in range(n_experts):
            lo, hi = fill_lo_sm[e], fill_hi_sm[e]

            def zstart(s, c):
                zero_copy(s).start()
                return c

            def zwait(s, c):
                zero_copy(s).wait()
                return c

            lax.fori_loop(lo, hi, zstart, 0)
            lax.fori_loop(lo, hi, zwait, 0)

        def tail_copy(b):
            s = pl.multiple_of(fill_hi_sm[n_experts - 1] + b * zrow.shape[0], zrow.shape[0])
            return pltpu.make_async_copy(zrow, buf_hbm.at[pl.ds(s, zrow.shape[0])], zsem)

        n_tail = (buf_hbm.shape[0] - fill_hi_sm[n_experts - 1]) // zrow.shape[0]

        def tstart(b, c):
            tail_copy(b).start()
            return c

        def twait(b, c):
            tail_copy(b).wait()
            return c

        lax.fori_loop(0, n_tail, tstart, 0)
        lax.fori_loop(0, n_tail, twait, 0)

    base = i * t

    def issue(j, c):
        tok = base + j
        for k in range(TOP_K):
            row_copy(tok, dest_sm[tok * TOP_K + k]).start()
        return c

    def drain(j, c):
        for k in range(TOP_K):
            row_copy(0, 0).wait()
        return c

    lax.fori_loop(0, t, issue, 0)
    lax.fori_loop(0, t, drain, 0)


def _dispatch(dest_flat, fill_lo, fill_hi, u, *, n_slots, n_experts, t):
    n, d = u.shape
    return pl.pallas_call(
        functools.partial(_dispatch_kernel, t=t, n_experts=n_experts),
        grid_spec=pltpu.PrefetchScalarGridSpec(
            num_scalar_prefetch=3,
            grid=(n // t,),
            in_specs=[pl.BlockSpec(memory_space=pl.ANY)],
            out_specs=pl.BlockSpec(memory_space=pl.ANY),
            scratch_shapes=[pltpu.VMEM((8, d), F32), pltpu.SemaphoreType.DMA(()), pltpu.SemaphoreType.DMA(())],
        ),
        out_shape=jax.ShapeDtypeStruct((n_slots, d), F32),
        compiler_params=_cparams(("arbitrary",)),
        name="moe_dispatch",
    )(dest_flat, fill_lo, fill_hi, u)


def _expert_kernel(be_sm, nu_sm, x_ref, w1g_ref, w1u_ref, b1g_ref, b1u_ref, w2_ref, b2_ref, o_ref, xb, acc):
    j = pl.program_id(0)
    c = pl.program_id(1)
    nc = pl.num_programs(1)

    @pl.when(j < nu_sm[0])
    def _():
        @pl.when(c == 0)
        def _():
            xb[...] = x_ref[...].astype(BF16)
            acc[...] = jnp.broadcast_to(b2_ref[...], acc.shape)

        x = xb[...]
        g = jnp.dot(x, w1g_ref[...].astype(BF16), preferred_element_type=F32) + b1g_ref[...]
        u = jnp.dot(x, w1u_ref[...].astype(BF16), preferred_element_type=F32) + b1u_ref[...]
        g = jnp.minimum(g, SWIGLU_LIMIT)
        u = jnp.clip(u, -SWIGLU_LIMIT, SWIGLU_LIMIT)
        act = g * jax.nn.sigmoid(SWIGLU_ALPHA * g) * (u + 1.0)
        acc[...] += jnp.dot(act.astype(BF16), w2_ref[...].astype(BF16), preferred_element_type=F32)

        @pl.when(c == nc - 1)
        def _():
            o_ref[...] = acc[...]

    @pl.when((j >= nu_sm[0]) & (c == nc - 1))
    def _():
        o_ref[...] = jnp.zeros_like(o_ref)


def _experts(blk_expert, n_used, buf, w1, b1, w2, b2, *, tm, tc):
    n_slots, d = buf.shape
    ne, _, two_de = w1.shape
    de = two_de // 2
    nc = de // tc
    nb = n_slots // tm

    def blk(j, nu):
        return jnp.minimum(j, nu[0] - 1)

    def chunk(j, c, nu):
        return jnp.where(j < nu[0], c, nc - 1)

    return pl.pallas_call(
        _expert_kernel,
        grid_spec=pltpu.PrefetchScalarGridSpec(
            num_scalar_prefetch=2,
            grid=(nb, nc),
            in_specs=[
                pl.BlockSpec((tm, d), lambda j, c, be, nu: (blk(j, nu), 0)),
                pl.BlockSpec((None, d, tc), lambda j, c, be, nu: (be[blk(j, nu)], 0, chunk(j, c, nu))),
                pl.BlockSpec((None, d, tc), lambda j, c, be, nu: (be[blk(j, nu)], 0, nc + chunk(j, c, nu))),
                pl.BlockSpec((None, 1, tc), lambda j, c, be, nu: (be[blk(j, nu)], 0, chunk(j, c, nu))),
                pl.BlockSpec((None, 1, tc), lambda j, c, be, nu: (be[blk(j, nu)], 0, nc + chunk(j, c, nu))),
                pl.BlockSpec((None, tc, d), lambda j, c, be, nu: (be[blk(j, nu)], chunk(j, c, nu), 0)),
                pl.BlockSpec((None, 1, d), lambda j, c, be, nu: (be[blk(j, nu)], 0, 0)),
            ],
            out_specs=pl.BlockSpec((tm, d), lambda j, c, be, nu: (j, 0)),
            scratch_shapes=[pltpu.VMEM((tm, d), BF16), pltpu.VMEM((tm, d), F32)],
        ),
        out_shape=jax.ShapeDtypeStruct((n_slots, d), F32),
        compiler_params=_cparams(("arbitrary", "arbitrary")),
        name="moe_experts",
    )(blk_expert, n_used, buf, w1, w1, b1, b1, w2, b2)


def _combine_kernel(dest_sm, gate_ref, h_ref, g_ref, y_hbm, o_ref, rows, sem, *, t):
    base = pl.program_id(0) * t

    def row_copy(slot, k, j):
        return pltpu.make_async_copy(y_hbm.at[pl.ds(slot, 1)], rows.at[k, pl.ds(j, 1)], sem)

    def issue(j, c):
        for k in range(TOP_K):
            row_copy(dest_sm[(base + j) * TOP_K + k], k, j).start()
        return c

    def drain(j, c):
        for k in range(TOP_K):
            row_copy(0, k, j).wait()
        return c

    lax.fori_loop(0, t, issue, 0)
    lax.fori_loop(0, t, drain, 0)
    gate = gate_ref[...]
    hf = h_ref[...]
    for k in range(TOP_K):
        hf = hf + gate[:, k:k + 1] * rows[k]
    o_ref[...] = _rms(hf, g_ref[...])


def _combine(dest_flat, gate, h, g, ybuf, *, t):
    n, d = h.shape
    return pl.pallas_call(
        functools.partial(_combine_kernel, t=t),
        grid_spec=pltpu.PrefetchScalarGridSpec(
            num_scalar_prefetch=1,
            grid=(n // t,),
            in_specs=[
                pl.BlockSpec((t, LANES), lambda i, ds: (i, 0)),
                pl.BlockSpec((t, d), lambda i, ds: (i, 0)),
                pl.BlockSpec((1, d), lambda i, ds: (0, 0)),
                pl.BlockSpec(memory_space=pl.ANY),
            ],
            out_specs=pl.BlockSpec((t, d), lambda i, ds: (i, 0)),
            scratch_shapes=[pltpu.VMEM((TOP_K, t, d), F32), pltpu.SemaphoreType.DMA(())],
        ),
        out_shape=jax.ShapeDtypeStruct((n, d), F32),
        compiler_params=_cparams(("arbitrary",)),
        name="moe_combine",
    )(dest_flat, gate, h, g, ybuf)


def _pad_lanes(v, lane0=0):
    out = jnp.zeros((1, LANES), F32)
    return out.at[0, lane0:lane0 + v.shape[0]].set(v.astype(F32))


def kernel(x, mem, ln_mix, w_in, conv_w, conv_b, dt_bias, a_log, d_skip, ssd_norm, fgate_bias, w_out, ln_xa, ln_mem,
           w_xq, w_xkv, w_xo, ln_ffn, w_router, b_router, w_moe1, b_moe1, w_moe2, b_moe2, ln_final):
    batch, seq, d = x.shape
    mem_len = mem.shape[1]
    depth = ln_mix.shape[0]
    n_ssd_heads = dt_bias.shape[1]
    n_fox_heads = fgate_bias.shape[1]
    d_ssd = n_ssd_heads * SSD_HEAD_DIM
    d_fox = n_fox_heads * FOX_HEAD_DIM
    cdim = d_ssd + 2 * SSD_GROUPS * SSD_STATE
    n_experts = w_router.shape[2]
    n = batch * seq
    fox_lane0 = n_ssd_heads
    assert n_ssd_heads + n_fox_heads <= LANES and cdim % d_ssd == 0 and d_ssd == d and d_fox == d

    tm = min(1024, n)
    o_z, o_xbc, o_dt = 0, d_ssd, d_ssd + cdim
    o_q = o_dt + n_ssd_heads
    o_k, o_v, o_f = o_q + d_fox, o_q + 2 * d_fox, o_q + 3 * d_fox
    q_col = (cdim + d_ssd) // FOX_HEAD_DIM
    k_col = q_col + n_fox_heads
    v_col = k_col + n_fox_heads

    head_of_ch = jnp.arange(d_ssd, dtype=I32) // SSD_HEAD_DIM
    e01 = (jnp.arange(LANES, dtype=I32)[:, None] == head_of_ch[None, :]).astype(BF16)

    n_assign = n * TOP_K
    nb = -(-n_assign // MOE_TM) + n_experts
    nb_pad = -(-nb // 8) * 8
    n_slots = nb * MOE_TM

    h = x.reshape(n, d)
    mem2 = mem.reshape(batch * mem_len, d)
    for i in range(depth):
        w = w_in[i]
        w_main = jnp.concatenate([w[:, o_xbc:o_dt], w[:, o_z:o_xbc], w[:, o_q:o_f]], axis=1).astype(BF16)
        w_small = jnp.concatenate(
            [w[:, o_dt:o_q], w[:, o_f:], jnp.zeros((d, LANES - n_ssd_heads - n_fox_heads), F32)], axis=1).astype(BF16)
        pmain, psmall = _inproj(h, ln_mix[i][None, :], w_main, w_small, tm, 1024)

        y_ssd, fcol, frow = _ssd(
            pmain, psmall, conv_w[i], conv_b[i][None, :], _pad_lanes(dt_bias[i]), _pad_lanes(a_log[i]),
            _pad_lanes(fgate_bias[i], fox_lane0), jnp.repeat(d_skip[i].astype(F32), SSD_HEAD_DIM)[None, :],
            ssd_norm[i][None, :], e01, batch=batch, seq=seq, d_inner=d_ssd, n_heads=n_ssd_heads, fox_lane0=fox_lane0)
        y_fox = _fox(pmain, fcol, frow, batch=batch, seq=seq, n_heads=n_fox_heads, q_col=q_col, k_col=k_col,
                     v_col=v_col, fox_lane0=fox_lane0, tq=min(256, seq))
        wo = w_out[i].astype(BF16)
        h = _mm_res([y_ssd, y_fox], [wo[:d_ssd], wo[d_ssd:]], h, tm, 512, "out_proj")

        q = _norm_matmul(h, ln_xa[i][None, :], w_xq[i].astype(BF16), tm, 1024, "xa_q")
        kv = _norm_matmul(mem2, ln_mem[i][None, :], w_xkv[i].astype(BF16), min(1024, batch * mem_len), 1024, "xa_kv")
        o = _xattn(q, kv, batch=batch, seq=seq, mem_len=mem_len, d_model=d, tq=min(512, seq))
        h = _mm_res([o], [w_xo[i].astype(BF16)], h, tm, 512, "xa_o")

        wr = jnp.concatenate([w_router[i], jnp.zeros((d, LANES - n_experts), F32)], axis=1)
        u, idx, gate, cnt = _router(h, ln_ffn[i][None, :], wr, _pad_lanes(b_router[i]), n_experts=n_experts,
                                    tm=min(512, n))
        dest, be, meta = _slots(idx, cnt, n_experts=n_experts, tm_e=MOE_TM, nb_pad=nb_pad, t=min(512, n))
        dest_flat = dest[:, :TOP_K].reshape(-1)
        fill_lo = meta[0, :n_experts] + meta[1, :n_experts]
        fill_hi = meta[2, :n_experts]
        buf = _dispatch(dest_flat, fill_lo, fill_hi, u, n_slots=n_slots, n_experts=n_experts, t=min(256, n))
        ybuf = _experts(be[:nb, 0], meta[3, :1], buf, w_moe1[i], b_moe1[i][:, None, :], w_moe2[i],
                        b_moe2[i][:, None, :], tm=MOE_TM, tc=MOE_TC)
        last = i == depth - 1
        assert last, "the fused final norm assumes a single layer"
        h = _combine(dest_flat, gate, h, ln_final[None, :], ybuf, t=min(256, n))
    return h.reshape(batch, seq, d)
```

```python
import functools

import jax
import jax.numpy as jnp
from jax import lax
from jax.experimental import pallas as pl
from jax.experimental.pallas import tpu as pltpu

F32 = jnp.float32
BF16 = jnp.bfloat16
I32 = jnp.int32

RMS_EPS = 1e-5
LANES = 128
SSD_HEAD_DIM = 64
SSD_GROUPS = 8
SSD_STATE = 128
SSD_CONV = 4
SSD_CHUNK = 128
CONV_HALO = 8
FOX_HEAD_DIM = 128
XA_HEADS = 4
TOP_K = 4
SWIGLU_LIMIT = 7.0
SWIGLU_ALPHA = 1.702
NEG_BIG = -1e30
VMEM_LIMIT = 56 * 1024 * 1024

MOE_TM = 512
MOE_TC = 512
DRAIN_UNROLL = 32


def _cparams(sem):
    return pltpu.CompilerParams(dimension_semantics=sem, vmem_limit_bytes=VMEM_LIMIT)


def _rms(x, g):
    ms = jnp.mean(x * x, axis=-1, keepdims=True)
    return x * lax.rsqrt(ms + RMS_EPS) * g


def _softplus(x):
    return jnp.maximum(x, 0.0) + jnp.log1p(jnp.exp(-jnp.abs(x)))


def _split3(x):
    hi = x.astype(BF16)
    r1 = x - hi.astype(F32)
    mid = r1.astype(BF16)
    lo = (r1 - mid.astype(F32)).astype(BF16)
    return hi, mid, lo


def _dot01_right(x, m01):
    hi, mid, lo = _split3(x)
    d = lambda a: jnp.dot(a, m01, preferred_element_type=F32)
    return d(hi) + d(mid) + d(lo)


def _dot01_left(m01, x):
    hi, mid, lo = _split3(x)
    d = lambda a: jnp.dot(m01, a, preferred_element_type=F32)
    return d(hi) + d(mid) + d(lo)


def _inproj_kernel(x_ref, g_ref, w_ref, ws_ref, o_ref, os_ref, u_scr):
    @pl.when(pl.program_id(1) == 0)
    def _():
        u = _rms(x_ref[...], g_ref[...]).astype(BF16)
        u_scr[...] = u
        os_ref[...] = jnp.dot(u, ws_ref[...], preferred_element_type=F32)

    o_ref[...] = jnp.dot(u_scr[...], w_ref[...], preferred_element_type=F32).astype(o_ref.dtype)


def _inproj(x, g, w_main, w_small, tm, tn):
    m, k = x.shape
    n = w_main.shape[1]
    return pl.pallas_call(
        _inproj_kernel,
        grid=(m // tm, n // tn),
        in_specs=[
            pl.BlockSpec((tm, k), lambda i, j: (i, 0)),
            pl.BlockSpec((1, k), lambda i, j: (0, 0)),
            pl.BlockSpec((k, tn), lambda i, j: (0, j)),
            pl.BlockSpec((k, LANES), lambda i, j: (0, 0)),
        ],
        out_specs=[
            pl.BlockSpec((tm, tn), lambda i, j: (i, j)),
            pl.BlockSpec((tm, LANES), lambda i, j: (i, 0)),
        ],
        out_shape=[jax.ShapeDtypeStruct((m, n), BF16), jax.ShapeDtypeStruct((m, LANES), F32)],
        scratch_shapes=[pltpu.VMEM((tm, k), BF16)],
        compiler_params=_cparams(("parallel", "arbitrary")),
        name="inproj",
    )(x, g, w_main, w_small)


def _norm_matmul_kernel(x_ref, g_ref, w_ref, o_ref, u_scr):
    @pl.when(pl.program_id(1) == 0)
    def _():
        u_scr[...] = _rms(x_ref[...], g_ref[...]).astype(BF16)

    o_ref[...] = jnp.dot(u_scr[...], w_ref[...], preferred_element_type=F32).astype(o_ref.dtype)


def _norm_matmul(x, g, w, tm, tn, name):
    m, k = x.shape
    n = w.shape[1]
    return pl.pallas_call(
        _norm_matmul_kernel,
        grid=(m // tm, n // tn),
        in_specs=[
            pl.BlockSpec((tm, k), lambda i, j: (i, 0)),
            pl.BlockSpec((1, k), lambda i, j: (0, 0)),
            pl.BlockSpec((k, tn), lambda i, j: (0, j)),
        ],
        out_specs=pl.BlockSpec((tm, tn), lambda i, j: (i, j)),
        out_shape=jax.ShapeDtypeStruct((m, n), BF16),
        scratch_shapes=[pltpu.VMEM((tm, k), BF16)],
        compiler_params=_cparams(("parallel", "arbitrary")),
        name=name,
    )(x, g, w)


def _mm_res_kernel(*refs, n_lhs):
    res_ref, o_ref = refs[2 * n_lhs], refs[2 * n_lhs + 1]
    acc = res_ref[...]
    for a_ref, w_ref in zip(refs[:n_lhs], refs[n_lhs:2 * n_lhs]):
        acc = acc + jnp.dot(a_ref[...], w_ref[...], preferred_element_type=F32)
    o_ref[...] = acc


def _mm_res(lhs, ws, res, tm, tn, name):
    m, n = res.shape
    n_lhs = len(lhs)
    in_specs = [pl.BlockSpec((tm, a.shape[1]), lambda i, j: (i, 0)) for a in lhs]
    in_specs += [pl.BlockSpec((w.shape[0], tn), lambda i, j: (0, j)) for w in ws]
    in_specs += [pl.BlockSpec((tm, tn), lambda i, j: (i, j))]
    return pl.pallas_call(
        functools.partial(_mm_res_kernel, n_lhs=n_lhs),
        grid=(m // tm, n // tn),
        in_specs=in_specs,
        out_specs=pl.BlockSpec((tm, tn), lambda i, j: (i, j)),
        out_shape=jax.ShapeDtypeStruct((m, n), F32),
        compiler_params=_cparams(("parallel", "arbitrary")),
        name=name,
    )(*lhs, *ws, res)


def _ssd_kernel(xbc_ref, z_ref, sm_ref, cw_ref, cb_ref, dtb_ref, alog_ref, fb_ref, dch_ref, gn_ref, e_ref,
                y_ref, fcol_ref, frow_ref,
                ext, cv, state, fcarry, yacc, dtch, csch, *, d_inner, n_heads, fox_lane0):
    L = SSD_CHUNK
    gw = d_inner // SSD_GROUPS
    hpg = gw // SSD_HEAD_DIM
    cdim = d_inner + 2 * SSD_GROUPS * SSD_STATE
    strip = 256

    @pl.when(pl.program_id(1) == 0)
    def _():
        ext[0:CONV_HALO, :] = jnp.zeros((CONV_HALO, cdim), F32)
        state[...] = jnp.zeros_like(state)
        fcarry[...] = jnp.zeros_like(fcarry)

    ext[CONV_HALO:CONV_HALO + L, :] = xbc_ref[...].astype(F32)
    for s0 in range(0, cdim, strip):
        acc = jnp.broadcast_to(cb_ref[:, s0:s0 + strip], (L, strip))
        for k in range(SSD_CONV):
            r0 = CONV_HALO - (SSD_CONV - 1) + k
            acc = acc + cw_ref[k:k + 1, s0:s0 + strip] * ext[r0:r0 + L, s0:s0 + strip]
        cv[:, s0:s0 + strip] = acc * jax.nn.sigmoid(acc)
    ext[0:CONV_HALO, :] = ext[L:L + CONV_HALO, :]

    row = lax.broadcasted_iota(I32, (L, L), 0)
    col = lax.broadcasted_iota(I32, (L, L), 1)
    causal = col <= row
    tri = causal.astype(BF16)

    sm = sm_ref[...]
    dt = _softplus(sm + dtb_ref[...])
    dta = dt * (-jnp.exp(alog_ref[...]))
    cs = _dot01_left(tri, dta)
    cst = cs.T
    logf = -_softplus(-(sm + fb_ref[...]))
    fc = _dot01_left(tri, logf) + fcarry[...]
    fcol_ref[...] = fc
    frow_ref[...] = fc.T
    fcarry[...] = fc[L - 1:L, :]

    e01 = e_ref[...]
    dtch[...] = _dot01_right(dt, e01)
    csch[...] = _dot01_right(cs, e01)

    lane_g = lax.broadcasted_iota(I32, (1, gw), 1)
    for g in range(SSD_GROUPS):
        c0 = g * gw
        bg = cv[:, d_inner + g * SSD_STATE:d_inner + (g + 1) * SSD_STATE]
        cg = cv[:, d_inner + SSD_GROUPS * SSD_STATE + g * SSD_STATE:d_inner + SSD_GROUPS * SSD_STATE + (g + 1) * SSD_STATE]
        cgb = cg.astype(BF16)
        cbm = lax.dot_general(cgb, bg.astype(BF16), (((1,), (1,)), ((), ())), preferred_element_type=F32)
        bgt = bg.T.astype(BF16)
        xg = cv[:, c0:c0 + gw]
        csg = csch[:, c0:c0 + gw]
        cs_last = csg[L - 1:L, :]
        xdt = xg * dtch[:, c0:c0 + gw]
        xw = (xdt * jnp.exp(cs_last - csg)).astype(BF16)
        st_new = jnp.dot(bgt, xw, preferred_element_type=F32)
        prev = state[g]
        yg = jnp.dot(cgb, prev.astype(BF16), preferred_element_type=F32) * jnp.exp(csg)
        state[g] = prev * jnp.exp(cs_last) + st_new
        yg = yg + xg * dch_ref[:, c0:c0 + gw]
        for r in range(hpg):
            h = g * hpg + r
            seg = cs[:, h:h + 1] - cst[h:h + 1, :]
            m = (cbm * jnp.exp(jnp.where(causal, seg, NEG_BIG))).astype(BF16)
            head = (lane_g >= r * SSD_HEAD_DIM) & (lane_g < (r + 1) * SSD_HEAD_DIM)
            xh = jnp.where(head, xdt, 0.0).astype(BF16)
            yg = yg + jnp.dot(m, xh, preferred_element_type=F32)
        yacc[:, c0:c0 + gw] = yg

    z = z_ref[...].astype(F32)
    gated = yacc[...] * (z * jax.nn.sigmoid(z))
    y_ref[...] = _rms(gated, gn_ref[...]).astype(y_ref.dtype)


def _ssd(pmain, psmall, conv_w, conv_b, dtb, alog, fb, dch, gnorm, e01, *, batch, seq, d_inner, n_heads, fox_lane0):
    nc = seq // SSD_CHUNK
    cdim = d_inner + 2 * SSD_GROUPS * SSD_STATE
    gw = d_inner // SSD_GROUPS
    m = batch * seq
    rowblk = lambda b, c: b * nc + c
    const = lambda shape: pl.BlockSpec(shape, lambda b, c: (0, 0))
    kern = functools.partial(_ssd_kernel, d_inner=d_inner, n_heads=n_heads, fox_lane0=fox_lane0)
    return pl.pallas_call(
        kern,
        grid=(batch, nc),
        in_specs=[
            pl.BlockSpec((SSD_CHUNK, cdim), lambda b, c: (rowblk(b, c), 0)),
            pl.BlockSpec((SSD_CHUNK, d_inner), lambda b, c: (rowblk(b, c), cdim // d_inner)),
            pl.BlockSpec((SSD_CHUNK, LANES), lambda b, c: (rowblk(b, c), 0)),
            const((SSD_CONV, cdim)), const((1, cdim)), const((1, LANES)), const((1, LANES)), const((1, LANES)),
            const((1, d_inner)), const((1, d_inner)), const((LANES, d_inner)),
        ],
        out_specs=[
            pl.BlockSpec((SSD_CHUNK, d_inner), lambda b, c: (rowblk(b, c), 0)),
            pl.BlockSpec((SSD_CHUNK, LANES), lambda b, c: (rowblk(b, c), 0)),
            pl.BlockSpec((None, LANES, SSD_CHUNK), lambda b, c: (b, 0, c)),
        ],
        out_shape=[
            jax.ShapeDtypeStruct((m, d_inner), BF16),
            jax.ShapeDtypeStruct((m, LANES), F32),
            jax.ShapeDtypeStruct((batch, LANES, seq), F32),
        ],
        scratch_shapes=[
            pltpu.VMEM((CONV_HALO + SSD_CHUNK, cdim), F32),
            pltpu.VMEM((SSD_CHUNK, cdim), F32),
            pltpu.VMEM((SSD_GROUPS, SSD_STATE, gw), F32),
            pltpu.VMEM((1, LANES), F32),
            pltpu.VMEM((SSD_CHUNK, d_inner), F32),
            pltpu.VMEM((SSD_CHUNK, d_inner), F32),
            pltpu.VMEM((SSD_CHUNK, d_inner), F32),
        ],
        compiler_params=_cparams(("arbitrary", "arbitrary")),
        name="ssd_scan",
    )(pmain, pmain, psmall, conv_w, conv_b, dtb, alog, fb, dch, gnorm, e01)


def _fox_kernel(q_ref, k_ref, v_ref, fcol_ref, frow_ref, o_ref, s_scr, fq_scr, m_scr, l_scr, acc_scr, *, tq, fox_lane0):
    h = pl.program_id(1)
    nq = q_ref.shape[0] // tq
    ng = tq // LANES
    scale = FOX_HEAD_DIM ** -0.5
    lane = lax.broadcasted_iota(I32, (tq, LANES), 1)
    causal = lax.broadcasted_iota(I32, (tq, tq), 1) <= lax.broadcasted_iota(I32, (tq, tq), 0)
    groups = lambda a: [a[:, g * LANES:(g + 1) * LANES] for g in range(ng)]

    for qi in range(nq):
        qs = qi * tq
        q = (q_ref[qs:qs + tq, :].astype(F32) * scale).astype(BF16)
        fq = jnp.sum(jnp.where(lane == fox_lane0 + h, fcol_ref[qs:qs + tq, :], 0.0), axis=-1, keepdims=True)
        fq_scr[...] = jnp.broadcast_to(fq, (tq, LANES))
        m_scr[...] = jnp.full((tq, LANES), NEG_BIG, F32)
        for kb in range(qi + 1):
            ks = kb * tq
            fk = frow_ref[pl.ds(fox_lane0 + h, 1), ks:ks + tq]
            s = lax.dot_general(q, k_ref[ks:ks + tq, :], (((1,), (1,)), ((), ())), preferred_element_type=F32)
            fqr = fq_scr[...]
            s = jnp.concatenate([sg + fqr for sg in groups(s)], axis=1) - fk
            if kb == qi:
                s = jnp.where(causal, s, NEG_BIG)
            s_scr[kb] = s
            m_scr[...] = functools.reduce(jnp.maximum, groups(s), m_scr[...])
        m_row = jnp.max(m_scr[...], axis=-1, keepdims=True)
        m_scr[...] = jnp.broadcast_to(m_row, (tq, LANES))
        l_scr[...] = jnp.zeros_like(l_scr)
        acc_scr[...] = jnp.zeros_like(acc_scr)
        for kb in range(qi + 1):
            ks = kb * tq
            mr = m_scr[...]
            ps = [jnp.exp(sg - mr) for sg in groups(s_scr[kb])]
            l_scr[...] += functools.reduce(jnp.add, ps)
            p = jnp.concatenate(ps, axis=1).astype(BF16)
            acc_scr[...] += jnp.dot(p, v_ref[ks:ks + tq, :], preferred_element_type=F32)
        l = jnp.sum(l_scr[...], axis=-1, keepdims=True)
        o_ref[qs:qs + tq, :] = (acc_scr[...] * (1.0 / l)).astype(o_ref.dtype)


def _fox(pmain, fcol, frow, *, batch, seq, n_heads, q_col, k_col, v_col, fox_lane0, tq):
    kern = functools.partial(_fox_kernel, tq=tq, fox_lane0=fox_lane0)
    head_block = lambda col0: pl.BlockSpec((seq, FOX_HEAD_DIM), lambda b, h: (b, col0 + h))
    return pl.pallas_call(
        kern,
        grid=(batch, n_heads),
        in_specs=[
            head_block(q_col), head_block(k_col), head_block(v_col),
            pl.BlockSpec((seq, LANES), lambda b, h: (b, 0)),
            pl.BlockSpec((None, LANES, seq), lambda b, h: (b, 0, 0)),
        ],
        out_specs=head_block(0),
        out_shape=jax.ShapeDtypeStruct((batch * seq, n_heads * FOX_HEAD_DIM), BF16),
        scratch_shapes=[
            pltpu.VMEM((seq // tq, tq, tq), F32),
            pltpu.VMEM((tq, LANES), F32),
            pltpu.VMEM((tq, LANES), F32),
            pltpu.VMEM((tq, LANES), F32),
            pltpu.VMEM((tq, FOX_HEAD_DIM), F32),
        ],
        compiler_params=_cparams(("parallel", "arbitrary")),
        name="fox_attn",
    )(pmain, pmain, pmain, fcol, frow)


def _xattn_kernel(q_ref, kv_ref, o_ref, *, d_model):
    hd = d_model // XA_HEADS
    scale = hd ** -0.5
    for h in range(XA_HEADS):
        q = q_ref[:, h * hd:(h + 1) * hd]
        k = kv_ref[:, h * hd:(h + 1) * hd]
        v = kv_ref[:, d_model + h * hd:d_model + (h + 1) * hd]
        s = lax.dot_general(q, k, (((1,), (1,)), ((), ())), preferred_element_type=F32) * scale
        p = jnp.exp(s - jnp.max(s, axis=-1, keepdims=True))
        p = p * (1.0 / jnp.sum(p, axis=-1, keepdims=True))
        o_ref[:, h * hd:(h + 1) * hd] = jnp.dot(p.astype(BF16), v, preferred_element_type=F32).astype(o_ref.dtype)


def _xattn(q, kv, *, batch, seq, mem_len, d_model, tq):
    nq = seq // tq
    return pl.pallas_call(
        functools.partial(_xattn_kernel, d_model=d_model),
        grid=(batch, nq),
        in_specs=[
            pl.BlockSpec((tq, d_model), lambda b, i: (b * nq + i, 0)),
            pl.BlockSpec((mem_len, 2 * d_model), lambda b, i: (b, 0)),
        ],
        out_specs=pl.BlockSpec((tq, d_model), lambda b, i: (b * nq + i, 0)),
        out_shape=jax.ShapeDtypeStruct((batch * seq, d_model), BF16),
        compiler_params=_cparams(("parallel", "arbitrary")),
        name="mem_xattn",
    )(q, kv)


def _router_kernel(h_ref, g_ref, wr_ref, br_ref, idx_ref, gate_ref, cnt_ref, *, n_experts):
    tm = h_ref.shape[0]

    @pl.when(pl.program_id(0) == 0)
    def _():
        cnt_ref[...] = jnp.zeros_like(cnt_ref)

    u = _rms(h_ref[...], g_ref[...])
    logits = jnp.dot(u, wr_ref[...], preferred_element_type=F32, precision=lax.Precision.HIGHEST) + br_ref[...]
    lane = lax.broadcasted_iota(I32, (tm, LANES), 1)
    work = jnp.where(lane < n_experts, logits, -jnp.inf)
    vals, idxs = [], []
    for _ in range(TOP_K):
        mx = jnp.max(work, axis=-1, keepdims=True)
        am = jnp.min(jnp.where(work == mx, lane, LANES), axis=-1, keepdims=True)
        vals.append(mx)
        idxs.append(am)
        work = jnp.where(lane == am, -jnp.inf, work)
    ex = [jnp.exp(v - vals[0]) for v in vals]
    den = ex[0] + ex[1] + ex[2] + ex[3]
    idx_out = jnp.zeros((tm, LANES), I32)
    gate_out = jnp.zeros((tm, LANES), F32)
    hits = jnp.zeros((tm, LANES), F32)
    for k in range(TOP_K):
        idx_out = jnp.where(lane == k, idxs[k], idx_out)
        gate_out = jnp.where(lane == k, ex[k] / den, gate_out)
        hits = hits + (lane == idxs[k]).astype(F32)
    idx_ref[...] = idx_out
    gate_ref[...] = gate_out
    cnt_ref[...] += jnp.broadcast_to(jnp.sum(hits, axis=0, keepdims=True), cnt_ref.shape)


def _router(h, g, wr, br, *, n_experts, tm):
    n, d = h.shape
    return pl.pallas_call(
        functools.partial(_router_kernel, n_experts=n_experts),
        grid=(n // tm,),
        in_specs=[
            pl.BlockSpec((tm, d), lambda i: (i, 0)),
            pl.BlockSpec((1, d), lambda i: (0, 0)),
            pl.BlockSpec((d, LANES), lambda i: (0, 0)),
            pl.BlockSpec((1, LANES), lambda i: (0, 0)),
        ],
        out_specs=[
            pl.BlockSpec((tm, LANES), lambda i: (i, 0)),
            pl.BlockSpec((tm, LANES), lambda i: (i, 0)),
            pl.BlockSpec((8, LANES), lambda i: (0, 0)),
        ],
        out_shape=[
            jax.ShapeDtypeStruct((n, LANES), I32),
            jax.ShapeDtypeStruct((n, LANES), F32),
            jax.ShapeDtypeStruct((8, LANES), F32),
        ],
        compiler_params=_cparams(("arbitrary",)),
        name="moe_router",
    )(h, g, wr, br)


def _slots_kernel(idx_ref, cnt_ref, dest_ref, be_ref, meta_ref, carry, *, n_experts, tm_e, nb_pad):
    t = idx_ref.shape[0]
    lane1 = lax.broadcasted_iota(I32, (1, LANES), 1)

    @pl.when(pl.program_id(0) == 0)
    def _():
        cnt = cnt_ref[0:1, :]
        padded = jnp.floor((cnt + (tm_e - 1)) / tm_e) * tm_e
        er = lax.broadcasted_iota(I32, (LANES, LANES), 0)
        ec = lax.broadcasted_iota(I32, (LANES, LANES), 1)
        start = _dot01_right(jnp.broadcast_to(padded, (8, LANES)), (er < ec).astype(BF16))[0:1, :]
        carry[...] = start
        pend = start + padded
        jrow = (lax.broadcasted_iota(I32, (nb_pad, LANES), 0) * tm_e).astype(F32)
        lane = lax.broadcasted_iota(I32, (nb_pad, LANES), 1)
        hit = jnp.where((pend <= jrow) & (lane < n_experts), 1.0, 0.0)
        be = jnp.minimum(jnp.sum(hit, axis=-1, keepdims=True), n_experts - 1.0)
        be_ref[...] = jnp.broadcast_to(be, (nb_pad, LANES)).astype(I32)
        total = jnp.sum(jnp.where(lane1 < n_experts, padded, 0.0), axis=-1, keepdims=True)
        meta = jnp.concatenate([start, cnt, pend, jnp.broadcast_to(total / tm_e, (1, LANES)),
                                jnp.zeros((4, LANES), F32)], axis=0)
        meta_ref[...] = meta.astype(I32)

    idx = idx_ref[...]
    lane = lax.broadcasted_iota(I32, (t, LANES), 1)
    ohs = [lane == idx[:, k:k + 1] for k in range(TOP_K)]
    oh = jnp.zeros((t, LANES), F32)
    for o in ohs:
        oh = oh + o.astype(F32)
    r = lax.broadcasted_iota(I32, (t, t), 0)
    c = lax.broadcasted_iota(I32, (t, t), 1)
    earlier = jnp.dot((c < r).astype(BF16), oh.astype(BF16), preferred_element_type=F32)
    base = carry[...] + earlier
    dest = jnp.zeros((t, LANES), F32)
    for k in range(TOP_K):
        dk = jnp.sum(jnp.where(ohs[k], base, 0.0), axis=-1, keepdims=True)
        dest = jnp.where(lane == k, dk, dest)
    dest_ref[...] = dest.astype(I32)
    carry[...] += jnp.sum(oh, axis=0, keepdims=True)


def _slots(idx, cnt, *, n_experts, tm_e, nb_pad, t):
    n = idx.shape[0]
    return pl.pallas_call(
        functools.partial(_slots_kernel, n_experts=n_experts, tm_e=tm_e, nb_pad=nb_pad),
        grid=(n // t,),
        in_specs=[pl.BlockSpec((t, LANES), lambda i: (i, 0)), pl.BlockSpec((8, LANES), lambda i: (0, 0))],
        out_specs=[
            pl.BlockSpec((t, LANES), lambda i: (i, 0)),
            pl.BlockSpec((nb_pad, LANES), lambda i: (0, 0)),
            pl.BlockSpec((8, LANES), lambda i: (0, 0)),
        ],
        out_shape=[
            jax.ShapeDtypeStruct((n, LANES), I32),
            jax.ShapeDtypeStruct((nb_pad, LANES), I32),
            jax.ShapeDtypeStruct((8, LANES), I32),
        ],
        scratch_shapes=[pltpu.VMEM((1, LANES), F32)],
        compiler_params=_cparams(("arbitrary",)),
        name="moe_slots",
    )(idx, cnt)


def _drain(copy, n):
    assert n % DRAIN_UNROLL == 0

    def body(_, c):
        for _ in range(DRAIN_UNROLL):
            copy.wait()
        return c

    lax.fori_loop(0, n // DRAIN_UNROLL, body, 0)


def _dispatch_kernel(dest_sm, fill_lo_sm, fill_hi_sm, h_ref, g_ref, buf_hbm, urows, zrow, sem, zsem, *, t, n_experts):
    i = pl.program_id(0)

    def row_copy(j, slot):
        return pltpu.make_async_copy(urows.at[pl.ds(j, 1)], buf_hbm.at[pl.ds(slot, 1)], sem)

    def zero_copy(slot):
        return pltpu.make_async_copy(zrow.at[pl.ds(0, 1)], buf_hbm.at[pl.ds(slot, 1)], zsem)

    @pl.when(i == 0)
    def _():
        zrow[...] = jnp.zeros_like(zrow)
        for e in range(n_experts):
            lo, hi = fill_lo_sm[e], fill_hi_sm[e]

            def zstart(s, c):
                zero_copy(s).start()
                return c

            def zwait(s, c):
                zero_copy(s).wait()
                return c

            lax.fori_loop(lo, hi, zstart, 0)
            lax.fori_loop(lo, hi, zwait, 0)

        def tail_copy(b):
            s = pl.multiple_of(fill_hi_sm[n_experts - 1] + b * zrow.shape[0], zrow.shape[0])
            return pltpu.make_async_copy(zrow, buf_hbm.at[pl.ds(s, zrow.shape[0])], zsem)

        n_tail = (buf_hbm.shape[0] - fill_hi_sm[n_experts - 1]) // zrow.shape[0]

        def tstart(b, c):
            tail_copy(b).start()
            return c

        def twait(b, c):
            tail_copy(b).wait()
            return c

        lax.fori_loop(0, n_tail, tstart, 0)
        lax.fori_loop(0, n_tail, twait, 0)

    base = i * t
    urows[...] = _rms(h_ref[...], g_ref[...])

    def issue(j, c):
        for k in range(TOP_K):
            row_copy(j, dest_sm[(base + j) * TOP_K + k]).start()
        return c

    lax.fori_loop(0, t, issue, 0)
    _drain(row_copy(0, 0), t * TOP_K)


def _dispatch(dest_flat, fill_lo, fill_hi, h, g, *, n_slots, n_experts, t):
    n, d = h.shape
    return pl.pallas_call(
        functools.partial(_dispatch_kernel, t=t, n_experts=n_experts),
        grid_spec=pltpu.PrefetchScalarGridSpec(
            num_scalar_prefetch=3,
            grid=(n // t,),
            in_specs=[pl.BlockSpec((t, d), lambda i, *_: (i, 0)), pl.BlockSpec((1, d), lambda i, *_: (0, 0))],
            out_specs=pl.BlockSpec(memory_space=pl.ANY),
            scratch_shapes=[pltpu.VMEM((t, d), F32), pltpu.VMEM((8, d), F32),
                            pltpu.SemaphoreType.DMA(()), pltpu.SemaphoreType.DMA(())],
        ),
        out_shape=jax.ShapeDtypeStruct((n_slots, d), F32),
        compiler_params=_cparams(("arbitrary",)),
        name="moe_dispatch",
    )(dest_flat, fill_lo, fill_hi, h, g)


def _expert_kernel(be_sm, nu_sm, x_ref, w1g_ref, w1u_ref, b1g_ref, b1u_ref, w2_ref, b2_ref, o_ref, xb, acc):
    j = pl.program_id(0)
    c = pl.program_id(1)
    nc = pl.num_programs(1)

    @pl.when(j < nu_sm[0])
    def _():
        @pl.when(c == 0)
        def _():
            xb[...] = x_ref[...].astype(BF16)
            acc[...] = jnp.broadcast_to(b2_ref[...], acc.shape)

        x = xb[...]
        g = jnp.dot(x, w1g_ref[...].astype(BF16), preferred_element_type=F32) + b1g_ref[...]
        u = jnp.dot(x, w1u_ref[...].astype(BF16), preferred_element_type=F32) + b1u_ref[...]
        g = jnp.minimum(g, SWIGLU_LIMIT)
        u = jnp.clip(u, -SWIGLU_LIMIT, SWIGLU_LIMIT)
        act = g * jax.nn.sigmoid(SWIGLU_ALPHA * g) * (u + 1.0)
        acc[...] += jnp.dot(act.astype(BF16), w2_ref[...].astype(BF16), preferred_element_type=F32)

        @pl.when(c == nc - 1)
        def _():
            o_ref[...] = acc[...]

    @pl.when((j >= nu_sm[0]) & (c == nc - 1))
    def _():
        o_ref[...] = jnp.zeros_like(o_ref)


def _experts(blk_expert, n_used, buf, w1, b1, w2, b2, *, tm, tc):
    n_slots, d = buf.shape
    ne, _, two_de = w1.shape
    de = two_de // 2
    nc = de // tc
    nb = n_slots // tm

    def blk(j, nu):
        return jnp.minimum(j, nu[0] - 1)

    def chunk(j, c, nu):
        return jnp.where(j < nu[0], c, nc - 1)

    return pl.pallas_call(
        _expert_kernel,
        grid_spec=pltpu.PrefetchScalarGridSpec(
            num_scalar_prefetch=2,
            grid=(nb, nc),
            in_specs=[
                pl.BlockSpec((tm, d), lambda j, c, be, nu: (blk(j, nu), 0)),
                pl.BlockSpec((None, d, tc), lambda j, c, be, nu: (be[blk(j, nu)], 0, chunk(j, c, nu))),
                pl.BlockSpec((None, d, tc), lambda j, c, be, nu: (be[blk(j, nu)], 0, nc + chunk(j, c, nu))),
                pl.BlockSpec((None, 1, tc), lambda j, c, be, nu: (be[blk(j, nu)], 0, chunk(j, c, nu))),
                pl.BlockSpec((None, 1, tc), lambda j, c, be, nu: (be[blk(j, nu)], 0, nc + chunk(j, c, nu))),
                pl.BlockSpec((None, tc, d), lambda j, c, be, nu: (be[blk(j, nu)], chunk(j, c, nu), 0)),
                pl.BlockSpec((None, 1, d), lambda j, c, be, nu: (be[blk(j, nu)], 0, 0)),
            ],
            out_specs=pl.BlockSpec((tm, d), lambda j, c, be, nu: (j, 0)),
            scratch_shapes=[pltpu.VMEM((tm, d), BF16), pltpu.VMEM((tm, d), F32)],
        ),
        out_shape=jax.ShapeDtypeStruct((n_slots, d), F32),
        compiler_params=_cparams(("arbitrary", "arbitrary")),
        name="moe_experts",
    )(blk_expert, n_used, buf, w1, w1, b1, b1, w2, b2)


def _combine_kernel(dest_sm, gate_ref, h_ref, g_ref, y_hbm, o_ref, rows, sem, *, t):
    base = pl.program_id(0) * t

    def row_copy(slot, k, j):
        return pltpu.make_async_copy(y_hbm.at[pl.ds(slot, 1)], rows.at[k, pl.ds(j, 1)], sem)

    def issue(j, c):
        for k in range(TOP_K):
            row_copy(dest_sm[(base + j) * TOP_K + k], k, j).start()
        return c

    lax.fori_loop(0, t, issue, 0)
    _drain(row_copy(0, 0, 0), t * TOP_K)
    gate = gate_ref[...]
    hf = h_ref[...]
    for k in range(TOP_K):
        hf = hf + gate[:, k:k + 1] * rows[k]
    o_ref[...] = _rms(hf, g_ref[...])


def _combine(dest_flat, gate, h, g, ybuf, *, t):
    n, d = h.shape
    return pl.pallas_call(
        functools.partial(_combine_kernel, t=t),
        grid_spec=pltpu.PrefetchScalarGridSpec(
            num_scalar_prefetch=1,
            grid=(n // t,),
            in_specs=[
                pl.BlockSpec((t, LANES), lambda i, ds: (i, 0)),
                pl.BlockSpec((t, d), lambda i, ds: (i, 0)),
                pl.BlockSpec((1, d), lambda i, ds: (0, 0)),
                pl.BlockSpec(memory_space=pl.ANY),
            ],
            out_specs=pl.BlockSpec((t, d), lambda i, ds: (i, 0)),
            scratch_shapes=[pltpu.VMEM((TOP_K, t, d), F32), pltpu.SemaphoreType.DMA(())],
        ),
        out_shape=jax.ShapeDtypeStruct((n, d), F32),
        compiler_params=_cparams(("arbitrary",)),
        name="moe_combine",
    )(dest_flat, gate, h, g, ybuf)


def _pad_lanes(v, lane0=0):
    out = jnp.zeros((1, LANES), F32)
    return out.at[0, lane0:lane0 + v.shape[0]].set(v.astype(F32))


def kernel(x, mem, ln_mix, w_in, conv_w, conv_b, dt_bias, a_log, d_skip, ssd_norm, fgate_bias, w_out, ln_xa, ln_mem,
           w_xq, w_xkv, w_xo, ln_ffn, w_router, b_router, w_moe1, b_moe1, w_moe2, b_moe2, ln_final):
    batch, seq, d = x.shape
    mem_len = mem.shape[1]
    depth = ln_mix.shape[0]
    n_ssd_heads = dt_bias.shape[1]
    n_fox_heads = fgate_bias.shape[1]
    d_ssd = n_ssd_heads * SSD_HEAD_DIM
    d_fox = n_fox_heads * FOX_HEAD_DIM
    cdim = d_ssd + 2 * SSD_GROUPS * SSD_STATE
    n_experts = w_router.shape[2]
    n = batch * seq
    fox_lane0 = n_ssd_heads
    assert n_ssd_heads + n_fox_heads <= LANES and cdim % d_ssd == 0 and d_ssd == d and d_fox == d

    tm = min(1024, n)
    o_z, o_xbc, o_dt = 0, d_ssd, d_ssd + cdim
    o_q = o_dt + n_ssd_heads
    o_k, o_v, o_f = o_q + d_fox, o_q + 2 * d_fox, o_q + 3 * d_fox
    q_col = (cdim + d_ssd) // FOX_HEAD_DIM
    k_col = q_col + n_fox_heads
    v_col = k_col + n_fox_heads

    head_of_ch = jnp.arange(d_ssd, dtype=I32) // SSD_HEAD_DIM
    e01 = (jnp.arange(LANES, dtype=I32)[:, None] == head_of_ch[None, :]).astype(BF16)

    n_assign = n * TOP_K
    nb = -(-n_assign // MOE_TM) + n_experts
    nb_pad = -(-nb // 8) * 8
    n_slots = nb * MOE_TM

    h = x.reshape(n, d)
    mem2 = mem.reshape(batch * mem_len, d)
    for i in range(depth):
        w = w_in[i]
        w_main = jnp.concatenate([w[:, o_xbc:o_dt], w[:, o_z:o_xbc], w[:, o_q:o_f]], axis=1).astype(BF16)
        w_small = jnp.concatenate(
            [w[:, o_dt:o_q], w[:, o_f:], jnp.zeros((d, LANES - n_ssd_heads - n_fox_heads), F32)], axis=1).astype(BF16)
        pmain, psmall = _inproj(h, ln_mix[i][None, :], w_main, w_small, tm, 1024)

        y_ssd, fcol, frow = _ssd(
            pmain, psmall, conv_w[i], conv_b[i][None, :], _pad_lanes(dt_bias[i]), _pad_lanes(a_log[i]),
            _pad_lanes(fgate_bias[i], fox_lane0), jnp.repeat(d_skip[i].astype(F32), SSD_HEAD_DIM)[None, :],
            ssd_norm[i][None, :], e01, batch=batch, seq=seq, d_inner=d_ssd, n_heads=n_ssd_heads, fox_lane0=fox_lane0)
        y_fox = _fox(pmain, fcol, frow, batch=batch, seq=seq, n_heads=n_fox_heads, q_col=q_col, k_col=k_col,
                     v_col=v_col, fox_lane0=fox_lane0, tq=min(256, seq))
        wo = w_out[i].astype(BF16)
        h = _mm_res([y_ssd, y_fox], [wo[:d_ssd], wo[d_ssd:]], h, tm, 512, "out_proj")

        q = _norm_matmul(h, ln_xa[i][None, :], w_xq[i].astype(BF16), tm, 1024, "xa_q")
        kv = _norm_matmul(mem2, ln_mem[i][None, :], w_xkv[i].astype(BF16), min(1024, batch * mem_len), 1024, "xa_kv")
        o = _xattn(q, kv, batch=batch, seq=seq, mem_len=mem_len, d_model=d, tq=min(512, seq))
        h = _mm_res([o], [w_xo[i].astype(BF16)], h, tm, 512, "xa_o")

        wr = jnp.concatenate([w_router[i], jnp.zeros((d, LANES - n_experts), F32)], axis=1)
        idx, gate, cnt = _router(h, ln_ffn[i][None, :], wr, _pad_lanes(b_router[i]), n_experts=n_experts,
                                 tm=min(512, n))
        dest, be, meta = _slots(idx, cnt, n_experts=n_experts, tm_e=MOE_TM, nb_pad=nb_pad, t=min(512, n))
        dest_flat = dest[:, :TOP_K].reshape(-1)
        fill_lo = meta[0, :n_experts] + meta[1, :n_experts]
        fill_hi = meta[2, :n_experts]
        buf = _dispatch(dest_flat, fill_lo, fill_hi, h, ln_ffn[i][None, :], n_slots=n_slots, n_experts=n_experts,
                        t=min(256, n))
        ybuf = _experts(be[:nb, 0], meta[3, :1], buf, w_moe1[i], b_moe1[i][:, None, :], w_moe2[i],
                        b_moe2[i][:, None, :], tm=MOE_TM, tc=MOE_TC)
        last = i == depth - 1
        assert last, "the fused final norm assumes a single layer"
        h = _combine(dest_flat, gate, h, ln_final[None, :], ybuf, t=min(256, n))
    return h.reshape(batch, seq, d)
```

```python
import functools

import jax
import jax.numpy as jnp
from jax import lax
from jax.experimental import pallas as pl
from jax.experimental.pallas import tpu as pltpu

F32 = jnp.float32
BF16 = jnp.bfloat16
I32 = jnp.int32
U32 = jnp.uint32

RMS_EPS = 1e-5
LANES = 128
SSD_HEAD_DIM = 64
SSD_GROUPS = 8
SSD_STATE = 128
SSD_CONV = 4
SSD_CHUNK = 128
CONV_HALO = 8
FOX_HEAD_DIM = 128
XA_HEADS = 4
TOP_K = 4
SWIGLU_LIMIT = 7.0
SWIGLU_ALPHA = 1.702
NEG_BIG = -1e30
VMEM_LIMIT = 56 * 1024 * 1024

MOE_TM = 512
MOE_TC = 256
DRAIN_UNROLL = 32


def _cparams(sem):
    return pltpu.CompilerParams(dimension_semantics=sem, vmem_limit_bytes=VMEM_LIMIT)


def _rms(x, g):
    ms = jnp.mean(x * x, axis=-1, keepdims=True)
    return x * lax.rsqrt(ms + RMS_EPS) * g


def _softplus(x):
    return jnp.maximum(x, 0.0) + jnp.log1p(jnp.exp(-jnp.abs(x)))


def _split3(x):
    hi = x.astype(BF16)
    r1 = x - hi.astype(F32)
    mid = r1.astype(BF16)
    lo = (r1 - mid.astype(F32)).astype(BF16)
    return hi, mid, lo


def _dot01_right(x, m01):
    hi, mid, lo = _split3(x)
    d = lambda a: jnp.dot(a, m01, preferred_element_type=F32)
    return d(hi) + d(mid) + d(lo)


def _dot01_left(m01, x):
    hi, mid, lo = _split3(x)
    d = lambda a: jnp.dot(m01, a, preferred_element_type=F32)
    return d(hi) + d(mid) + d(lo)


def _inproj_kernel(x_ref, g_ref, w_ref, ws_ref, o_ref, os_ref, u_scr):
    @pl.when(pl.program_id(1) == 0)
    def _():
        u = _rms(x_ref[...], g_ref[...]).astype(BF16)
        u_scr[...] = u
        os_ref[...] = jnp.dot(u, ws_ref[...], preferred_element_type=F32)

    o_ref[...] = jnp.dot(u_scr[...], w_ref[...], preferred_element_type=F32).astype(o_ref.dtype)


def _inproj(x, g, w_main, w_small, tm, tn):
    m, k = x.shape
    n = w_main.shape[1]
    return pl.pallas_call(
        _inproj_kernel,
        grid=(m // tm, n // tn),
        in_specs=[
            pl.BlockSpec((tm, k), lambda i, j: (i, 0)),
            pl.BlockSpec((1, k), lambda i, j: (0, 0)),
            pl.BlockSpec((k, tn), lambda i, j: (0, j)),
            pl.BlockSpec((k, LANES), lambda i, j: (0, 0)),
        ],
        out_specs=[
            pl.BlockSpec((tm, tn), lambda i, j: (i, j)),
            pl.BlockSpec((tm, LANES), lambda i, j: (i, 0)),
        ],
        out_shape=[jax.ShapeDtypeStruct((m, n), BF16), jax.ShapeDtypeStruct((m, LANES), F32)],
        scratch_shapes=[pltpu.VMEM((tm, k), BF16)],
        compiler_params=_cparams(("parallel", "arbitrary")),
        name="inproj",
    )(x, g, w_main, w_small)


def _norm_matmul_kernel(x_ref, g_ref, w_ref, o_ref, u_scr):
    @pl.when(pl.program_id(1) == 0)
    def _():
        u_scr[...] = _rms(x_ref[...], g_ref[...]).astype(BF16)

    o_ref[...] = jnp.dot(u_scr[...], w_ref[...], preferred_element_type=F32).astype(o_ref.dtype)


def _norm_matmul(x, g, w, tm, tn, name):
    m, k = x.shape
    n = w.shape[1]
    return pl.pallas_call(
        _norm_matmul_kernel,
        grid=(m // tm, n // tn),
        in_specs=[
            pl.BlockSpec((tm, k), lambda i, j: (i, 0)),
            pl.BlockSpec((1, k), lambda i, j: (0, 0)),
            pl.BlockSpec((k, tn), lambda i, j: (0, j)),
        ],
        out_specs=pl.BlockSpec((tm, tn), lambda i, j: (i, j)),
        out_shape=jax.ShapeDtypeStruct((m, n), BF16),
        scratch_shapes=[pltpu.VMEM((tm, k), BF16)],
        compiler_params=_cparams(("parallel", "arbitrary")),
        name=name,
    )(x, g, w)


def _mm_res_kernel(*refs, n_lhs):
    res_ref, o_ref = refs[2 * n_lhs], refs[2 * n_lhs + 1]
    acc = res_ref[...]
    for a_ref, w_ref in zip(refs[:n_lhs], refs[n_lhs:2 * n_lhs]):
        acc = acc + jnp.dot(a_ref[...], w_ref[...], preferred_element_type=F32)
    o_ref[...] = acc


def _mm_res(lhs, w, res, tm, tn, name):
    m, n = res.shape
    n_lhs = len(lhs)
    kb = lhs[0].shape[1]
    assert all(a.shape[1] == kb for a in lhs) and w.shape[0] == n_lhs * kb
    ws = [w] * n_lhs
    in_specs = [pl.BlockSpec((tm, kb), lambda i, j: (i, 0)) for _ in lhs]
    in_specs += [pl.BlockSpec((kb, tn), lambda i, j, r=r: (r, j)) for r in range(n_lhs)]
    in_specs += [pl.BlockSpec((tm, tn), lambda i, j: (i, j))]
    return pl.pallas_call(
        functools.partial(_mm_res_kernel, n_lhs=n_lhs),
        grid=(m // tm, n // tn),
        in_specs=in_specs,
        out_specs=pl.BlockSpec((tm, tn), lambda i, j: (i, j)),
        out_shape=jax.ShapeDtypeStruct((m, n), F32),
        compiler_params=_cparams(("parallel", "arbitrary")),
        name=name,
    )(*lhs, *ws, res)


def _ssd_kernel(xbc_ref, z_ref, sm_ref, cw_ref, cb_ref, dtb_ref, alog_ref, fb_ref, dch_ref, gn_ref, e_ref,
                y_ref, fcol_ref, frow_ref,
                ext, cv, state, fcarry, yacc, dtch, csch, *, d_inner, n_heads, fox_lane0):
    L = SSD_CHUNK
    gw = d_inner // SSD_GROUPS
    hpg = gw // SSD_HEAD_DIM
    cdim = d_inner + 2 * SSD_GROUPS * SSD_STATE
    strip = 256

    @pl.when(pl.program_id(1) == 0)
    def _():
        ext[0:CONV_HALO, :] = jnp.zeros((CONV_HALO, cdim), F32)
        state[...] = jnp.zeros_like(state)
        fcarry[...] = jnp.zeros_like(fcarry)

    ext[CONV_HALO:CONV_HALO + L, :] = xbc_ref[...].astype(F32)
    for s0 in range(0, cdim, strip):
        acc = jnp.broadcast_to(cb_ref[:, s0:s0 + strip], (L, strip))
        for k in range(SSD_CONV):
            r0 = CONV_HALO - (SSD_CONV - 1) + k
            acc = acc + cw_ref[k:k + 1, s0:s0 + strip] * ext[r0:r0 + L, s0:s0 + strip]
        cv[:, s0:s0 + strip] = acc * jax.nn.sigmoid(acc)
    ext[0:CONV_HALO, :] = ext[L:L + CONV_HALO, :]

    row = lax.broadcasted_iota(I32, (L, L), 0)
    col = lax.broadcasted_iota(I32, (L, L), 1)
    causal = col <= row
    tri = causal.astype(BF16)

    sm = sm_ref[...]
    dt = _softplus(sm + dtb_ref[...])
    dta = dt * (-jnp.exp(alog_ref[...]))
    cs = _dot01_left(tri, dta)
    cst = cs.T
    logf = -_softplus(-(sm + fb_ref[...]))
    fc = _dot01_left(tri, logf) + fcarry[...]
    fcol_ref[...] = fc
    frow_ref[...] = fc.T
    fcarry[...] = fc[L - 1:L, :]

    e01 = e_ref[...]
    dtch[...] = _dot01_right(dt, e01)
    csch[...] = _dot01_right(cs, e01)

    lane_g = lax.broadcasted_iota(I32, (1, gw), 1)
    for g in range(SSD_GROUPS):
        c0 = g * gw
        bg = cv[:, d_inner + g * SSD_STATE:d_inner + (g + 1) * SSD_STATE]
        cg = cv[:, d_inner + SSD_GROUPS * SSD_STATE + g * SSD_STATE:d_inner + SSD_GROUPS * SSD_STATE + (g + 1) * SSD_STATE]
        cgb = cg.astype(BF16)
        cbm = lax.dot_general(cgb, bg.astype(BF16), (((1,), (1,)), ((), ())), preferred_element_type=F32)
        bgt = bg.T.astype(BF16)
        xg = cv[:, c0:c0 + gw]
        csg = csch[:, c0:c0 + gw]
        cs_last = csg[L - 1:L, :]
        xdt = xg * dtch[:, c0:c0 + gw]
        xw = (xdt * jnp.exp(cs_last - csg)).astype(BF16)
        st_new = jnp.dot(bgt, xw, preferred_element_type=F32)
        prev = state[g]
        yg = jnp.dot(cgb, prev.astype(BF16), preferred_element_type=F32) * jnp.exp(csg)
        state[g] = prev * jnp.exp(cs_last) + st_new
        yg = yg + xg * dch_ref[:, c0:c0 + gw]
        for r in range(hpg):
            h = g * hpg + r
            seg = cs[:, h:h + 1] - cst[h:h + 1, :]
            m = (cbm * jnp.exp(jnp.where(causal, seg, NEG_BIG))).astype(BF16)
            head = (lane_g >= r * SSD_HEAD_DIM) & (lane_g < (r + 1) * SSD_HEAD_DIM)
            xh = jnp.where(head, xdt, 0.0).astype(BF16)
            yg = yg + jnp.dot(m, xh, preferred_element_type=F32)
        yacc[:, c0:c0 + gw] = yg

    z = z_ref[...].astype(F32)
    gated = yacc[...] * (z * jax.nn.sigmoid(z))
    y_ref[...] = _rms(gated, gn_ref[...]).astype(y_ref.dtype)


def _ssd(pmain, psmall, conv_w, conv_b, dtb, alog, fb, dch, gnorm, e01, *, batch, seq, d_inner, n_heads, fox_lane0):
    nc = seq // SSD_CHUNK
    cdim = d_inner + 2 * SSD_GROUPS * SSD_STATE
    gw = d_inner // SSD_GROUPS
    m = batch * seq
    rowblk = lambda b, c: b * nc + c
    const = lambda shape: pl.BlockSpec(shape, lambda b, c: (0, 0))
    kern = functools.partial(_ssd_kernel, d_inner=d_inner, n_heads=n_heads, fox_lane0=fox_lane0)
    return pl.pallas_call(
        kern,
        grid=(batch, nc),
        in_specs=[
            pl.BlockSpec((SSD_CHUNK, cdim), lambda b, c: (rowblk(b, c), 0)),
            pl.BlockSpec((SSD_CHUNK, d_inner), lambda b, c: (rowblk(b, c), cdim // d_inner)),
            pl.BlockSpec((SSD_CHUNK, LANES), lambda b, c: (rowblk(b, c), 0)),
            const((SSD_CONV, cdim)), const((1, cdim)), const((1, LANES)), const((1, LANES)), const((1, LANES)),
            const((1, d_inner)), const((1, d_inner)), const((LANES, d_inner)),
        ],
        out_specs=[
            pl.BlockSpec((SSD_CHUNK, d_inner), lambda b, c: (rowblk(b, c), 0)),
            pl.BlockSpec((SSD_CHUNK, LANES), lambda b, c: (rowblk(b, c), 0)),
            pl.BlockSpec((None, LANES, SSD_CHUNK), lambda b, c: (b, 0, c)),
        ],
        out_shape=[
            jax.ShapeDtypeStruct((m, d_inner), BF16),
            jax.ShapeDtypeStruct((m, LANES), F32),
            jax.ShapeDtypeStruct((batch, LANES, seq), F32),
        ],
        scratch_shapes=[
            pltpu.VMEM((CONV_HALO + SSD_CHUNK, cdim), F32),
            pltpu.VMEM((SSD_CHUNK, cdim), F32),
            pltpu.VMEM((SSD_GROUPS, SSD_STATE, gw), F32),
            pltpu.VMEM((1, LANES), F32),
            pltpu.VMEM((SSD_CHUNK, d_inner), F32),
            pltpu.VMEM((SSD_CHUNK, d_inner), F32),
            pltpu.VMEM((SSD_CHUNK, d_inner), F32),
        ],
        compiler_params=_cparams(("arbitrary", "arbitrary")),
        name="ssd_scan",
    )(pmain, pmain, psmall, conv_w, conv_b, dtb, alog, fb, dch, gnorm, e01)


def _fox_kernel(q_ref, k_ref, v_ref, fcol_ref, frow_ref, o_ref, s_scr, fq_scr, m_scr, l_scr, acc_scr, *, tq, fox_lane0):
    h = pl.program_id(1)
    nq = q_ref.shape[0] // tq
    ng = tq // LANES
    scale = FOX_HEAD_DIM ** -0.5
    lane = lax.broadcasted_iota(I32, (tq, LANES), 1)
    causal = lax.broadcasted_iota(I32, (tq, tq), 1) <= lax.broadcasted_iota(I32, (tq, tq), 0)
    groups = lambda a: [a[:, g * LANES:(g + 1) * LANES] for g in range(ng)]

    for qi in range(nq):
        qs = qi * tq
        q = (q_ref[qs:qs + tq, :].astype(F32) * scale).astype(BF16)
        fq = jnp.sum(jnp.where(lane == fox_lane0 + h, fcol_ref[qs:qs + tq, :], 0.0), axis=-1, keepdims=True)
        fq_scr[...] = jnp.broadcast_to(fq, (tq, LANES))
        m_scr[...] = jnp.full((tq, LANES), NEG_BIG, F32)
        for kb in range(qi + 1):
            ks = kb * tq
            fk = frow_ref[pl.ds(fox_lane0 + h, 1), ks:ks + tq]
            s = lax.dot_general(q, k_ref[ks:ks + tq, :], (((1,), (1,)), ((), ())), preferred_element_type=F32)
            fqr = fq_scr[...]
            s = jnp.concatenate([sg + fqr for sg in groups(s)], axis=1) - fk
            if kb == qi:
                s = jnp.where(causal, s, NEG_BIG)
            s_scr[kb] = s
            m_scr[...] = functools.reduce(jnp.maximum, groups(s), m_scr[...])
        m_row = jnp.max(m_scr[...], axis=-1, keepdims=True)
        m_scr[...] = jnp.broadcast_to(m_row, (tq, LANES))
        l_scr[...] = jnp.zeros_like(l_scr)
        acc_scr[...] = jnp.zeros_like(acc_scr)
        for kb in range(qi + 1):
            ks = kb * tq
            mr = m_scr[...]
            ps = [jnp.exp(sg - mr) for sg in groups(s_scr[kb])]
            l_scr[...] += functools.reduce(jnp.add, ps)
            p = jnp.concatenate(ps, axis=1).astype(BF16)
            acc_scr[...] += jnp.dot(p, v_ref[ks:ks + tq, :], preferred_element_type=F32)
        l = jnp.sum(l_scr[...], axis=-1, keepdims=True)
        o_ref[qs:qs + tq, :] = (acc_scr[...] * (1.0 / l)).astype(o_ref.dtype)


def _fox(pmain, fcol, frow, *, batch, seq, n_heads, q_col, k_col, v_col, fox_lane0, tq):
    kern = functools.partial(_fox_kernel, tq=tq, fox_lane0=fox_lane0)
    head_block = lambda col0: pl.BlockSpec((seq, FOX_HEAD_DIM), lambda b, h: (b, col0 + h))
    return pl.pallas_call(
        kern,
        grid=(batch, n_heads),
        in_specs=[
            head_block(q_col), head_block(k_col), head_block(v_col),
            pl.BlockSpec((seq, LANES), lambda b, h: (b, 0)),
            pl.BlockSpec((None, LANES, seq), lambda b, h: (b, 0, 0)),
        ],
        out_specs=head_block(0),
        out_shape=jax.ShapeDtypeStruct((batch * seq, n_heads * FOX_HEAD_DIM), BF16),
        scratch_shapes=[
            pltpu.VMEM((seq // tq, tq, tq), F32),
            pltpu.VMEM((tq, LANES), F32),
            pltpu.VMEM((tq, LANES), F32),
            pltpu.VMEM((tq, LANES), F32),
            pltpu.VMEM((tq, FOX_HEAD_DIM), F32),
        ],
        compiler_params=_cparams(("parallel", "arbitrary")),
        name="fox_attn",
    )(pmain, pmain, pmain, fcol, frow)


def _xattn_kernel(q_ref, kv_ref, o_ref, *, d_model):
    hd = d_model // XA_HEADS
    scale = hd ** -0.5
    for h in range(XA_HEADS):
        q = q_ref[:, h * hd:(h + 1) * hd]
        k = kv_ref[:, h * hd:(h + 1) * hd]
        v = kv_ref[:, d_model + h * hd:d_model + (h + 1) * hd]
        s = lax.dot_general(q, k, (((1,), (1,)), ((), ())), preferred_element_type=F32) * scale
        p = jnp.exp(s - jnp.max(s, axis=-1, keepdims=True))
        p = p * (1.0 / jnp.sum(p, axis=-1, keepdims=True))
        o_ref[:, h * hd:(h + 1) * hd] = jnp.dot(p.astype(BF16), v, preferred_element_type=F32).astype(o_ref.dtype)


def _xattn(q, kv, *, batch, seq, mem_len, d_model, tq):
    nq = seq // tq
    return pl.pallas_call(
        functools.partial(_xattn_kernel, d_model=d_model),
        grid=(batch, nq),
        in_specs=[
            pl.BlockSpec((tq, d_model), lambda b, i: (b * nq + i, 0)),
            pl.BlockSpec((mem_len, 2 * d_model), lambda b, i: (b, 0)),
        ],
        out_specs=pl.BlockSpec((tq, d_model), lambda b, i: (b * nq + i, 0)),
        out_shape=jax.ShapeDtypeStruct((batch * seq, d_model), BF16),
        compiler_params=_cparams(("parallel", "arbitrary")),
        name="mem_xattn",
    )(q, kv)


def _router_kernel(h_ref, g_ref, wr_ref, br_ref, idx_ref, gate_ref, cnt_ref, *, n_experts):
    tm = h_ref.shape[0]

    @pl.when(pl.program_id(0) == 0)
    def _():
        cnt_ref[...] = jnp.zeros_like(cnt_ref)

    u = _rms(h_ref[...], g_ref[...])
    logits = jnp.dot(u, wr_ref[...], preferred_element_type=F32, precision=lax.Precision.HIGHEST) + br_ref[...]
    lane = lax.broadcasted_iota(I32, (tm, LANES), 1)
    work = jnp.where(lane < n_experts, logits, -jnp.inf)
    vals, idxs = [], []
    for _ in range(TOP_K):
        mx = jnp.max(work, axis=-1, keepdims=True)
        am = jnp.min(jnp.where(work == mx, lane, LANES), axis=-1, keepdims=True)
        vals.append(mx)
        idxs.append(am)
        work = jnp.where(lane == am, -jnp.inf, work)
    ex = [jnp.exp(v - vals[0]) for v in vals]
    den = ex[0] + ex[1] + ex[2] + ex[3]
    idx_out = jnp.zeros((tm, LANES), I32)
    gate_out = jnp.zeros((tm, LANES), F32)
    hits = jnp.zeros((tm, LANES), F32)
    for k in range(TOP_K):
        idx_out = jnp.where(lane == k, idxs[k], idx_out)
        gate_out = jnp.where(lane == k, ex[k] / den, gate_out)
        hits = hits + (lane == idxs[k]).astype(F32)
    idx_ref[...] = idx_out
    gate_ref[...] = gate_out
    cnt_ref[...] += jnp.broadcast_to(jnp.sum(hits, axis=0, keepdims=True), cnt_ref.shape)


def _router(h, g, wr, br, *, n_experts, tm):
    n, d = h.shape
    return pl.pallas_call(
        functools.partial(_router_kernel, n_experts=n_experts),
        grid=(n // tm,),
        in_specs=[
            pl.BlockSpec((tm, d), lambda i: (i, 0)),
            pl.BlockSpec((1, d), lambda i: (0, 0)),
            pl.BlockSpec((d, LANES), lambda i: (0, 0)),
            pl.BlockSpec((1, LANES), lambda i: (0, 0)),
        ],
        out_specs=[
            pl.BlockSpec((tm, LANES), lambda i: (i, 0)),
            pl.BlockSpec((tm, LANES), lambda i: (i, 0)),
            pl.BlockSpec((8, LANES), lambda i: (0, 0)),
        ],
        out_shape=[
            jax.ShapeDtypeStruct((n, LANES), I32),
            jax.ShapeDtypeStruct((n, LANES), F32),
            jax.ShapeDtypeStruct((8, LANES), F32),
        ],
        compiler_params=_cparams(("arbitrary",)),
        name="moe_router",
    )(h, g, wr, br)


def _slots_kernel(idx_ref, cnt_ref, dest_ref, be_ref, meta_ref, carry, *, n_experts, tm_e, nb_pad):
    t = idx_ref.shape[0]
    lane1 = lax.broadcasted_iota(I32, (1, LANES), 1)

    @pl.when(pl.program_id(0) == 0)
    def _():
        cnt = cnt_ref[0:1, :]
        padded = jnp.floor((cnt + (tm_e - 1)) / tm_e) * tm_e
        er = lax.broadcasted_iota(I32, (LANES, LANES), 0)
        ec = lax.broadcasted_iota(I32, (LANES, LANES), 1)
        start = _dot01_right(jnp.broadcast_to(padded, (8, LANES)), (er < ec).astype(BF16))[0:1, :]
        carry[...] = start
        pend = start + padded
        jrow = (lax.broadcasted_iota(I32, (nb_pad, LANES), 0) * tm_e).astype(F32)
        lane = lax.broadcasted_iota(I32, (nb_pad, LANES), 1)
        hit = jnp.where((pend <= jrow) & (lane < n_experts), 1.0, 0.0)
        be = jnp.minimum(jnp.sum(hit, axis=-1, keepdims=True), n_experts - 1.0)
        be_ref[...] = jnp.broadcast_to(be, (nb_pad, LANES)).astype(I32)
        total = jnp.sum(jnp.where(lane1 < n_experts, padded, 0.0), axis=-1, keepdims=True)
        meta = jnp.concatenate([start, cnt, pend, jnp.broadcast_to(total / tm_e, (1, LANES)),
                                jnp.zeros((4, LANES), F32)], axis=0)
        meta_ref[...] = meta.astype(I32)

    idx = idx_ref[...]
    lane = lax.broadcasted_iota(I32, (t, LANES), 1)
    ohs = [lane == idx[:, k:k + 1] for k in range(TOP_K)]
    oh = jnp.zeros((t, LANES), F32)
    for o in ohs:
        oh = oh + o.astype(F32)
    r = lax.broadcasted_iota(I32, (t, t), 0)
    c = lax.broadcasted_iota(I32, (t, t), 1)
    earlier = jnp.dot((c < r).astype(BF16), oh.astype(BF16), preferred_element_type=F32)
    base = carry[...] + earlier
    dest = jnp.zeros((t, LANES), F32)
    for k in range(TOP_K):
        dk = jnp.sum(jnp.where(ohs[k], base, 0.0), axis=-1, keepdims=True)
        dest = jnp.where(lane == k, dk, dest)
    dest_ref[...] = dest.astype(I32)
    carry[...] += jnp.sum(oh, axis=0, keepdims=True)


def _slots(idx, cnt, *, n_experts, tm_e, nb_pad, t):
    n = idx.shape[0]
    return pl.pallas_call(
        functools.partial(_slots_kernel, n_experts=n_experts, tm_e=tm_e, nb_pad=nb_pad),
        grid=(n // t,),
        in_specs=[pl.BlockSpec((t, LANES), lambda i: (i, 0)), pl.BlockSpec((8, LANES), lambda i: (0, 0))],
        out_specs=[
            pl.BlockSpec((t, LANES), lambda i: (i, 0)),
            pl.BlockSpec((nb_pad, LANES), lambda i: (0, 0)),
            pl.BlockSpec((8, LANES), lambda i: (0, 0)),
        ],
        out_shape=[
            jax.ShapeDtypeStruct((n, LANES), I32),
            jax.ShapeDtypeStruct((nb_pad, LANES), I32),
            jax.ShapeDtypeStruct((8, LANES), I32),
        ],
        scratch_shapes=[pltpu.VMEM((1, LANES), F32)],
        compiler_params=_cparams(("arbitrary",)),
        name="moe_slots",
    )(idx, cnt)


def _drain(copy, n):
    assert n % DRAIN_UNROLL == 0

    def body(_, c):
        for _ in range(DRAIN_UNROLL):
            copy.wait()
        return c

    lax.fori_loop(0, n // DRAIN_UNROLL, body, 0)


def _dispatch_kernel(dest_sm, fill_lo_sm, fill_hi_sm, h_ref, g_ref, buf_hbm, urows, zrow, sem, zsem, *, t, n_experts):
    i = pl.program_id(0)

    def row_copy(j, slot):
        return pltpu.make_async_copy(urows.at[pl.ds(j, 1)], buf_hbm.at[pl.ds(slot, 1)], sem)

    def zero_copy(slot):
        return pltpu.make_async_copy(zrow.at[pl.ds(0, 1)], buf_hbm.at[pl.ds(slot, 1)], zsem)

    @pl.when(i == 0)
    def _():
        zrow[...] = jnp.zeros_like(zrow)
        for e in range(n_experts):
            lo, hi = fill_lo_sm[e], fill_hi_sm[e]

            def zstart(s, c):
                zero_copy(s).start()
                return c

            def zwait(s, c):
                zero_copy(s).wait()
                return c

            lax.fori_loop(lo, hi, zstart, 0)
            lax.fori_loop(lo, hi, zwait, 0)

        def tail_copy(b):
            s = pl.multiple_of(fill_hi_sm[n_experts - 1] + b * zrow.shape[0], zrow.shape[0])
            return pltpu.make_async_copy(zrow, buf_hbm.at[pl.ds(s, zrow.shape[0])], zsem)

        n_tail = (buf_hbm.shape[0] - fill_hi_sm[n_experts - 1]) // zrow.shape[0]

        def tstart(b, c):
            tail_copy(b).start()
            return c

        def twait(b, c):
            tail_copy(b).wait()
            return c

        lax.fori_loop(0, n_tail, tstart, 0)
        lax.fori_loop(0, n_tail, twait, 0)

    base = i * t
    urows[...] = _pack_bf16_pairs(_rms(h_ref[...], g_ref[...]))

    def issue(j, c):
        for k in range(TOP_K):
            row_copy(j, dest_sm[(base + j) * TOP_K + k]).start()
        return c

    lax.fori_loop(0, t, issue, 0)
    _drain(row_copy(0, 0), t * TOP_K)


def _dispatch(dest_flat, fill_lo, fill_hi, h, g, *, n_slots, n_experts, t):
    n, d = h.shape
    return pl.pallas_call(
        functools.partial(_dispatch_kernel, t=t, n_experts=n_experts),
        grid_spec=pltpu.PrefetchScalarGridSpec(
            num_scalar_prefetch=3,
            grid=(n // t,),
            in_specs=[pl.BlockSpec((t, d), lambda i, *_: (i, 0)), pl.BlockSpec((1, d), lambda i, *_: (0, 0))],
            out_specs=pl.BlockSpec(memory_space=pl.ANY),
            scratch_shapes=[pltpu.VMEM((t, d // 2), U32), pltpu.VMEM((8, d // 2), U32),
                            pltpu.SemaphoreType.DMA(()), pltpu.SemaphoreType.DMA(())],
        ),
        out_shape=jax.ShapeDtypeStruct((n_slots, d // 2), U32),
        compiler_params=_cparams(("arbitrary",)),
        name="moe_dispatch",
    )(dest_flat, fill_lo, fill_hi, h, g)


def _pack_bf16_pairs(x):
    w = x.shape[1] // 2
    bits = lambda a: lax.bitcast_convert_type(a.astype(BF16).astype(F32), U32)
    return (bits(x[:, :w]) >> 16) | (bits(x[:, w:]) & jnp.uint32(0xFFFF0000))


def _unpack_bf16_pairs(p):
    lo = lax.bitcast_convert_type(p << 16, F32)
    hi = lax.bitcast_convert_type(p & jnp.uint32(0xFFFF0000), F32)
    return lo, hi


def _expert_kernel(be_sm, nu_sm, x_ref, w1g_ref, w1u_ref, b1g_ref, b1u_ref, w2_ref, b2_ref, o_ref,
                   xb, acc, w1g_res, w1u_res, w2_res):
    j = pl.program_id(0)
    c = pl.program_id(1)
    nc = pl.num_programs(1)
    half = xb.shape[1] // 2

    @pl.when(j < nu_sm[0])
    def _():
        @pl.when(_first_block_of_expert(j, be_sm))
        def _():
            w1g_res[c] = w1g_ref[...].astype(BF16)
            w1u_res[c] = w1u_ref[...].astype(BF16)
            w2_res[c] = w2_ref[...].astype(BF16)

        @pl.when(c == 0)
        def _():
            lo, hi = _unpack_bf16_pairs(x_ref[...])
            xb[:, :half] = lo.astype(BF16)
            xb[:, half:] = hi.astype(BF16)
            acc[...] = jnp.broadcast_to(b2_ref[...], acc.shape)

        x = xb[...]
        g = jnp.dot(x, w1g_res[c], preferred_element_type=F32) + b1g_ref[...]
        u = jnp.dot(x, w1u_res[c], preferred_element_type=F32) + b1u_ref[...]
        g = jnp.minimum(g, SWIGLU_LIMIT)
        u = jnp.clip(u, -SWIGLU_LIMIT, SWIGLU_LIMIT)
        act = g * jax.nn.sigmoid(SWIGLU_ALPHA * g) * (u + 1.0)
        acc[...] += jnp.dot(act.astype(BF16), w2_res[c], preferred_element_type=F32)

        @pl.when(c == nc - 1)
        def _():
            o_ref[...] = _pack_bf16_pairs(acc[...])

    @pl.when((j >= nu_sm[0]) & (c == nc - 1))
    def _():
        o_ref[...] = jnp.zeros_like(o_ref)


def _first_block_of_expert(j, be_sm):
    return (j == 0) | (be_sm[j] != be_sm[jnp.maximum(j - 1, 0)])


def _experts(blk_expert, n_used, buf, w1, b1, w2, b2, *, tm, tc):
    n_slots, half = buf.shape
    d = 2 * half
    ne, _, two_de = w1.shape
    de = two_de // 2
    nc = de // tc
    nb = n_slots // tm

    def blk(j, nu):
        return jnp.minimum(j, nu[0] - 1)

    def wchunk(j, c, be, nu):
        stream = (j < nu[0]) & _first_block_of_expert(j, be)
        return jnp.where(stream, c, nc - 1)

    def bchunk(j, c, nu):
        return jnp.where(j < nu[0], c, nc - 1)

    return pl.pallas_call(
        _expert_kernel,
        grid_spec=pltpu.PrefetchScalarGridSpec(
            num_scalar_prefetch=2,
            grid=(nb, nc),
            in_specs=[
                pl.BlockSpec((tm, half), lambda j, c, be, nu: (blk(j, nu), 0)),
                pl.BlockSpec((None, d, tc), lambda j, c, be, nu: (be[blk(j, nu)], 0, wchunk(j, c, be, nu))),
                pl.BlockSpec((None, d, tc), lambda j, c, be, nu: (be[blk(j, nu)], 0, nc + wchunk(j, c, be, nu))),
                pl.BlockSpec((None, 1, tc), lambda j, c, be, nu: (be[blk(j, nu)], 0, bchunk(j, c, nu))),
                pl.BlockSpec((None, 1, tc), lambda j, c, be, nu: (be[blk(j, nu)], 0, nc + bchunk(j, c, nu))),
                pl.BlockSpec((None, tc, d), lambda j, c, be, nu: (be[blk(j, nu)], wchunk(j, c, be, nu), 0)),
                pl.BlockSpec((None, 1, d), lambda j, c, be, nu: (be[blk(j, nu)], 0, 0)),
            ],
            out_specs=pl.BlockSpec((tm, half), lambda j, c, be, nu: (j, 0)),
            scratch_shapes=[
                pltpu.VMEM((tm, d), BF16), pltpu.VMEM((tm, d), F32),
                pltpu.VMEM((nc, d, tc), BF16), pltpu.VMEM((nc, d, tc), BF16), pltpu.VMEM((nc, tc, d), BF16),
            ],
        ),
        out_shape=jax.ShapeDtypeStruct((n_slots, half), U32),
        compiler_params=_cparams(("arbitrary", "arbitrary")),
        name="moe_experts",
    )(blk_expert, n_used, buf, w1, w1, b1, b1, w2, b2)


def _combine_kernel(dest_sm, gate_ref, h_ref, g_ref, y_hbm, o_ref, rows, sem, *, t):
    base = pl.program_id(0) * t

    def row_copy(slot, k, j):
        return pltpu.make_async_copy(y_hbm.at[pl.ds(slot, 1)], rows.at[k, pl.ds(j, 1)], sem)

    def issue(j, c):
        for k in range(TOP_K):
            row_copy(dest_sm[(base + j) * TOP_K + k], k, j).start()
        return c

    lax.fori_loop(0, t, issue, 0)
    _drain(row_copy(0, 0, 0), t * TOP_K)
    gate = gate_ref[...]
    half = h_ref.shape[1] // 2
    y_lo = h_ref[:, :half]
    y_hi = h_ref[:, half:]
    for k in range(TOP_K):
        lo, hi = _unpack_bf16_pairs(rows[k])
        y_lo = y_lo + gate[:, k:k + 1] * lo
        y_hi = y_hi + gate[:, k:k + 1] * hi
    o_ref[...] = _rms(jnp.concatenate([y_lo, y_hi], axis=1), g_ref[...])


def _combine(dest_flat, gate, h, g, ybuf, *, t):
    n, d = h.shape
    return pl.pallas_call(
        functools.partial(_combine_kernel, t=t),
        grid_spec=pltpu.PrefetchScalarGridSpec(
            num_scalar_prefetch=1,
            grid=(n // t,),
            in_specs=[
                pl.BlockSpec((t, LANES), lambda i, ds: (i, 0)),
                pl.BlockSpec((t, d), lambda i, ds: (i, 0)),
                pl.BlockSpec((1, d), lambda i, ds: (0, 0)),
                pl.BlockSpec(memory_space=pl.ANY),
            ],
            out_specs=pl.BlockSpec((t, d), lambda i, ds: (i, 0)),
            scratch_shapes=[pltpu.VMEM((TOP_K, t, d // 2), U32), pltpu.SemaphoreType.DMA(())],
        ),
        out_shape=jax.ShapeDtypeStruct((n, d), F32),
        compiler_params=_cparams(("arbitrary",)),
        name="moe_combine",
    )(dest_flat, gate, h, g, ybuf)


def _pad_lanes(v, lane0=0):
    out = jnp.zeros((1, LANES), F32)
    return out.at[0, lane0:lane0 + v.shape[0]].set(v.astype(F32))


def kernel(x, mem, ln_mix, w_in, conv_w, conv_b, dt_bias, a_log, d_skip, ssd_norm, fgate_bias, w_out, ln_xa, ln_mem,
           w_xq, w_xkv, w_xo, ln_ffn, w_router, b_router, w_moe1, b_moe1, w_moe2, b_moe2, ln_final):
    batch, seq, d = x.shape
    mem_len = mem.shape[1]
    depth = ln_mix.shape[0]
    n_ssd_heads = dt_bias.shape[1]
    n_fox_heads = fgate_bias.shape[1]
    d_ssd = n_ssd_heads * SSD_HEAD_DIM
    d_fox = n_fox_heads * FOX_HEAD_DIM
    cdim = d_ssd + 2 * SSD_GROUPS * SSD_STATE
    n_experts = w_router.shape[2]
    n = batch * seq
    fox_lane0 = n_ssd_heads
    assert n_ssd_heads + n_fox_heads <= LANES and cdim % d_ssd == 0 and d_ssd == d and d_fox == d

    tm = min(1024, n)
    o_z, o_xbc, o_dt = 0, d_ssd, d_ssd + cdim
    o_q = o_dt + n_ssd_heads
    o_k, o_v, o_f = o_q + d_fox, o_q + 2 * d_fox, o_q + 3 * d_fox
    q_col = (cdim + d_ssd) // FOX_HEAD_DIM
    k_col = q_col + n_fox_heads
    v_col = k_col + n_fox_heads

    head_of_ch = jnp.arange(d_ssd, dtype=I32) // SSD_HEAD_DIM
    e01 = (jnp.arange(LANES, dtype=I32)[:, None] == head_of_ch[None, :]).astype(BF16)

    n_assign = n * TOP_K
    nb = -(-n_assign // MOE_TM) + n_experts
    nb_pad = -(-nb // 8) * 8
    n_slots = nb * MOE_TM

    h = x.reshape(n, d)
    mem2 = mem.reshape(batch * mem_len, d)
    for i in range(depth):
        w = w_in[i]
        w_main = jnp.concatenate([w[:, o_xbc:o_dt], w[:, o_z:o_xbc], w[:, o_q:o_f]], axis=1).astype(BF16)
        w_small = jnp.concatenate(
            [w[:, o_dt:o_q], w[:, o_f:], jnp.zeros((d, LANES - n_ssd_heads - n_fox_heads), F32)], axis=1).astype(BF16)
        pmain, psmall = _inproj(h, ln_mix[i][None, :], w_main, w_small, tm, 1024)

        y_ssd, fcol, frow = _ssd(
            pmain, psmall, conv_w[i], conv_b[i][None, :], _pad_lanes(dt_bias[i]), _pad_lanes(a_log[i]),
            _pad_lanes(fgate_bias[i], fox_lane0), jnp.repeat(d_skip[i].astype(F32), SSD_HEAD_DIM)[None, :],
            ssd_norm[i][None, :], e01, batch=batch, seq=seq, d_inner=d_ssd, n_heads=n_ssd_heads, fox_lane0=fox_lane0)
        y_fox = _fox(pmain, fcol, frow, batch=batch, seq=seq, n_heads=n_fox_heads, q_col=q_col, k_col=k_col,
                     v_col=v_col, fox_lane0=fox_lane0, tq=min(256, seq))
        h = _mm_res([y_ssd, y_fox], w_out[i].astype(BF16), h, tm, 512, "out_proj")

        q = _norm_matmul(h, ln_xa[i][None, :], w_xq[i].astype(BF16), tm, 1024, "xa_q")
        kv = _norm_matmul(mem2, ln_mem[i][None, :], w_xkv[i].astype(BF16), min(1024, batch * mem_len), 1024, "xa_kv")
        o = _xattn(q, kv, batch=batch, seq=seq, mem_len=mem_len, d_model=d, tq=min(512, seq))
        h = _mm_res([o], w_xo[i].astype(BF16), h, tm, 512, "xa_o")

        wr = jnp.concatenate([w_router[i], jnp.zeros((d, LANES - n_experts), F32)], axis=1)
        idx, gate, cnt = _router(h, ln_ffn[i][None, :], wr, _pad_lanes(b_router[i]), n_experts=n_experts,
                                 tm=min(512, n))
        dest, be, meta = _slots(idx, cnt, n_experts=n_experts, tm_e=MOE_TM, nb_pad=nb_pad, t=min(512, n))
        dest_flat = dest[:, :TOP_K].reshape(-1)
        fill_lo = meta[0, :n_experts] + meta[1, :n_experts]
        fill_hi = meta[2, :n_experts]
        buf = _dispatch(dest_flat, fill_lo, fill_hi, h, ln_ffn[i][None, :], n_slots=n_slots, n_experts=n_experts,
                        t=min(256, n))
        ybuf = _experts(be[:nb, 0], meta[3, :1], buf, w_moe1[i], b_moe1[i][:, None, :], w_moe2[i],
                        b_moe2[i][:, None, :], tm=MOE_TM, tc=MOE_TC)
        last = i == depth - 1
        assert last, "the fused final norm assumes a single layer"
        h = _combine(dest_flat, gate, h, ln_final[None, :], ybuf, t=min(256, n))
    return h.reshape(batch, seq, d)
```

```python
import functools

import jax
import jax.numpy as jnp
from jax import lax
from jax.experimental import pallas as pl
from jax.experimental.pallas import tpu as pltpu

F32 = jnp.float32
BF16 = jnp.bfloat16
I32 = jnp.int32
U32 = jnp.uint32

RMS_EPS = 1e-5
LANES = 128
SSD_HEAD_DIM = 64
SSD_GROUPS = 8
SSD_STATE = 128
SSD_CONV = 4
SSD_CHUNK = 128
CONV_HALO = 8
FOX_HEAD_DIM = 128
XA_HEADS = 4
TOP_K = 4
SWIGLU_LIMIT = 7.0
SWIGLU_ALPHA = 1.702
NEG_BIG = -1e30
VMEM_LIMIT = 56 * 1024 * 1024

MOE_TM = 512
MOE_TC = 256
DRAIN_UNROLL = 32
ISSUE_GROUP = 8


def _cparams(sem):
    return pltpu.CompilerParams(dimension_semantics=sem, vmem_limit_bytes=VMEM_LIMIT)


def _rms(x, g):
    ms = jnp.mean(x * x, axis=-1, keepdims=True)
    return x * lax.rsqrt(ms + RMS_EPS) * g


def _softplus(x):
    return jnp.maximum(x, 0.0) + jnp.log1p(jnp.exp(-jnp.abs(x)))


def _split3(x):
    hi = x.astype(BF16)
    r1 = x - hi.astype(F32)
    mid = r1.astype(BF16)
    lo = (r1 - mid.astype(F32)).astype(BF16)
    return hi, mid, lo


def _dot01_right(x, m01):
    hi, mid, lo = _split3(x)
    d = lambda a: jnp.dot(a, m01, preferred_element_type=F32)
    return d(hi) + d(mid) + d(lo)


def _dot01_left(m01, x):
    hi, mid, lo = _split3(x)
    d = lambda a: jnp.dot(m01, a, preferred_element_type=F32)
    return d(hi) + d(mid) + d(lo)


def _inproj_kernel(x_ref, g_ref, w_ref, ws_ref, o_ref, os_ref, u_scr):
    @pl.when(pl.program_id(1) == 0)
    def _():
        u = _rms(x_ref[...], g_ref[...]).astype(BF16)
        u_scr[...] = u
        os_ref[...] = jnp.dot(u, ws_ref[...], preferred_element_type=F32)

    o_ref[...] = jnp.dot(u_scr[...], w_ref[...], preferred_element_type=F32).astype(o_ref.dtype)


def _inproj(x, g, w_main, w_small, tm, tn):
    m, k = x.shape
    n = w_main.shape[1]
    return pl.pallas_call(
        _inproj_kernel,
        grid=(m // tm, n // tn),
        in_specs=[
            pl.BlockSpec((tm, k), lambda i, j: (i, 0)),
            pl.BlockSpec((1, k), lambda i, j: (0, 0)),
            pl.BlockSpec((k, tn), lambda i, j: (0, j)),
            pl.BlockSpec((k, LANES), lambda i, j: (0, 0)),
        ],
        out_specs=[
            pl.BlockSpec((tm, tn), lambda i, j: (i, j)),
            pl.BlockSpec((tm, LANES), lambda i, j: (i, 0)),
        ],
        out_shape=[jax.ShapeDtypeStruct((m, n), BF16), jax.ShapeDtypeStruct((m, LANES), F32)],
        scratch_shapes=[pltpu.VMEM((tm, k), BF16)],
        compiler_params=_cparams(("parallel", "arbitrary")),
        name="inproj",
    )(x, g, w_main, w_small)


def _norm_matmul_kernel(x_ref, g_ref, w_ref, o_ref, u_scr):
    @pl.when(pl.program_id(1) == 0)
    def _():
        u_scr[...] = _rms(x_ref[...], g_ref[...]).astype(BF16)

    o_ref[...] = jnp.dot(u_scr[...], w_ref[...], preferred_element_type=F32).astype(o_ref.dtype)


def _norm_matmul(x, g, w, tm, tn, name):
    m, k = x.shape
    n = w.shape[1]
    return pl.pallas_call(
        _norm_matmul_kernel,
        grid=(m // tm, n // tn),
        in_specs=[
            pl.BlockSpec((tm, k), lambda i, j: (i, 0)),
            pl.BlockSpec((1, k), lambda i, j: (0, 0)),
            pl.BlockSpec((k, tn), lambda i, j: (0, j)),
        ],
        out_specs=pl.BlockSpec((tm, tn), lambda i, j: (i, j)),
        out_shape=jax.ShapeDtypeStruct((m, n), BF16),
        scratch_shapes=[pltpu.VMEM((tm, k), BF16)],
        compiler_params=_cparams(("parallel", "arbitrary")),
        name=name,
    )(x, g, w)


def _mm_res_kernel(*refs, n_lhs):
    res_ref, o_ref = refs[2 * n_lhs], refs[2 * n_lhs + 1]
    acc = res_ref[...]
    for a_ref, w_ref in zip(refs[:n_lhs], refs[n_lhs:2 * n_lhs]):
        acc = acc + jnp.dot(a_ref[...], w_ref[...], preferred_element_type=F32)
    o_ref[...] = acc


def _mm_res(lhs, w, res, tm, tn, name):
    m, n = res.shape
    n_lhs = len(lhs)
    kb = lhs[0].shape[1]
    assert all(a.shape[1] == kb for a in lhs) and w.shape[0] == n_lhs * kb
    ws = [w] * n_lhs
    in_specs = [pl.BlockSpec((tm, kb), lambda i, j: (i, 0)) for _ in lhs]
    in_specs += [pl.BlockSpec((kb, tn), lambda i, j, r=r: (r, j)) for r in range(n_lhs)]
    in_specs += [pl.BlockSpec((tm, tn), lambda i, j: (i, j))]
    return pl.pallas_call(
        functools.partial(_mm_res_kernel, n_lhs=n_lhs),
        grid=(m // tm, n // tn),
        in_specs=in_specs,
        out_specs=pl.BlockSpec((tm, tn), lambda i, j: (i, j)),
        out_shape=jax.ShapeDtypeStruct((m, n), F32),
        compiler_params=_cparams(("parallel", "arbitrary")),
        name=name,
    )(*lhs, *ws, res)


def _ssd_kernel(xbc_ref, z_ref, sm_ref, cw_ref, cb_ref, dtb_ref, alog_ref, fb_ref, dch_ref, gn_ref, e_ref,
                y_ref, fcol_ref, frow_ref,
                ext, cv, state, fcarry, yacc, dtch, csch, *, d_inner, n_heads, fox_lane0):
    L = SSD_CHUNK
    gw = d_inner // SSD_GROUPS
    hpg = gw // SSD_HEAD_DIM
    cdim = d_inner + 2 * SSD_GROUPS * SSD_STATE
    strip = 256

    @pl.when(pl.program_id(1) == 0)
    def _():
        ext[0:CONV_HALO, :] = jnp.zeros((CONV_HALO, cdim), F32)
        state[...] = jnp.zeros_like(state)
        fcarry[...] = jnp.zeros_like(fcarry)

    ext[CONV_HALO:CONV_HALO + L, :] = xbc_ref[...].astype(F32)
    for s0 in range(0, cdim, strip):
        acc = jnp.broadcast_to(cb_ref[:, s0:s0 + strip], (L, strip))
        for k in range(SSD_CONV):
            r0 = CONV_HALO - (SSD_CONV - 1) + k
            acc = acc + cw_ref[k:k + 1, s0:s0 + strip] * ext[r0:r0 + L, s0:s0 + strip]
        cv[:, s0:s0 + strip] = acc * jax.nn.sigmoid(acc)
    ext[0:CONV_HALO, :] = ext[L:L + CONV_HALO, :]

    row = lax.broadcasted_iota(I32, (L, L), 0)
    col = lax.broadcasted_iota(I32, (L, L), 1)
    causal = col <= row
    tri = causal.astype(BF16)

    sm = sm_ref[...]
    dt = _softplus(sm + dtb_ref[...])
    dta = dt * (-jnp.exp(alog_ref[...]))
    cs = _dot01_left(tri, dta)
    cst = cs.T
    logf = -_softplus(-(sm + fb_ref[...]))
    fc = _dot01_left(tri, logf) + fcarry[...]
    fcol_ref[...] = fc
    frow_ref[...] = fc.T
    fcarry[...] = fc[L - 1:L, :]

    e01 = e_ref[...]
    dtch[...] = _dot01_right(dt, e01)
    csch[...] = _dot01_right(cs, e01)

    lane_g = lax.broadcasted_iota(I32, (1, gw), 1)
    for g in range(SSD_GROUPS):
        c0 = g * gw
        bg = cv[:, d_inner + g * SSD_STATE:d_inner + (g + 1) * SSD_STATE]
        cg = cv[:, d_inner + SSD_GROUPS * SSD_STATE + g * SSD_STATE:d_inner + SSD_GROUPS * SSD_STATE + (g + 1) * SSD_STATE]
        cgb = cg.astype(BF16)
        cbm = lax.dot_general(cgb, bg.astype(BF16), (((1,), (1,)), ((), ())), preferred_element_type=F32)
        bgt = bg.T.astype(BF16)
        xg = cv[:, c0:c0 + gw]
        csg = csch[:, c0:c0 + gw]
        cs_last = csg[L - 1:L, :]
        xdt = xg * dtch[:, c0:c0 + gw]
        xw = (xdt * jnp.exp(cs_last - csg)).astype(BF16)
        st_new = jnp.dot(bgt, xw, preferred_element_type=F32)
        prev = state[g]
        yg = jnp.dot(cgb, prev.astype(BF16), preferred_element_type=F32) * jnp.exp(csg)
        state[g] = prev * jnp.exp(cs_last) + st_new
        yg = yg + xg * dch_ref[:, c0:c0 + gw]
        for r in range(hpg):
            h = g * hpg + r
            seg = cs[:, h:h + 1] - cst[h:h + 1, :]
            m = (cbm * jnp.exp(jnp.where(causal, seg, NEG_BIG))).astype(BF16)
            head = (lane_g >= r * SSD_HEAD_DIM) & (lane_g < (r + 1) * SSD_HEAD_DIM)
            xh = jnp.where(head, xdt, 0.0).astype(BF16)
            yg = yg + jnp.dot(m, xh, preferred_element_type=F32)
        yacc[:, c0:c0 + gw] = yg

    z = z_ref[...].astype(F32)
    gated = yacc[...] * (z * jax.nn.sigmoid(z))
    y_ref[...] = _rms(gated, gn_ref[...]).astype(y_ref.dtype)


def _ssd(pmain, psmall, conv_w, conv_b, dtb, alog, fb, dch, gnorm, e01, *, batch, seq, d_inner, n_heads, fox_lane0):
    nc = seq // SSD_CHUNK
    cdim = d_inner + 2 * SSD_GROUPS * SSD_STATE
    gw = d_inner // SSD_GROUPS
    m = batch * seq
    rowblk = lambda b, c: b * nc + c
    const = lambda shape: pl.BlockSpec(shape, lambda b, c: (0, 0))
    kern = functools.partial(_ssd_kernel, d_inner=d_inner, n_heads=n_heads, fox_lane0=fox_lane0)
    return pl.pallas_call(
        kern,
        grid=(batch, nc),
        in_specs=[
            pl.BlockSpec((SSD_CHUNK, cdim), lambda b, c: (rowblk(b, c), 0)),
            pl.BlockSpec((SSD_CHUNK, d_inner), lambda b, c: (rowblk(b, c), cdim // d_inner)),
            pl.BlockSpec((SSD_CHUNK, LANES), lambda b, c: (rowblk(b, c), 0)),
            const((SSD_CONV, cdim)), const((1, cdim)), const((1, LANES)), const((1, LANES)), const((1, LANES)),
            const((1, d_inner)), const((1, d_inner)), const((LANES, d_inner)),
        ],
        out_specs=[
            pl.BlockSpec((SSD_CHUNK, d_inner), lambda b, c: (rowblk(b, c), 0)),
            pl.BlockSpec((SSD_CHUNK, LANES), lambda b, c: (rowblk(b, c), 0)),
            pl.BlockSpec((None, LANES, SSD_CHUNK), lambda b, c: (b, 0, c)),
        ],
        out_shape=[
            jax.ShapeDtypeStruct((m, d_inner), BF16),
            jax.ShapeDtypeStruct((m, LANES), F32),
            jax.ShapeDtypeStruct((batch, LANES, seq), F32),
        ],
        scratch_shapes=[
            pltpu.VMEM((CONV_HALO + SSD_CHUNK, cdim), F32),
            pltpu.VMEM((SSD_CHUNK, cdim), F32),
            pltpu.VMEM((SSD_GROUPS, SSD_STATE, gw), F32),
            pltpu.VMEM((1, LANES), F32),
            pltpu.VMEM((SSD_CHUNK, d_inner), F32),
            pltpu.VMEM((SSD_CHUNK, d_inner), F32),
            pltpu.VMEM((SSD_CHUNK, d_inner), F32),
        ],
        compiler_params=_cparams(("arbitrary", "arbitrary")),
        name="ssd_scan",
    )(pmain, pmain, psmall, conv_w, conv_b, dtb, alog, fb, dch, gnorm, e01)


def _fox_kernel(q_ref, k_ref, v_ref, fcol_ref, frow_ref, o_ref, s_scr, fq_scr, m_scr, l_scr, acc_scr, *, tq, fox_lane0):
    h = pl.program_id(1)
    nq = q_ref.shape[0] // tq
    ng = tq // LANES
    scale = FOX_HEAD_DIM ** -0.5
    lane = lax.broadcasted_iota(I32, (tq, LANES), 1)
    causal = lax.broadcasted_iota(I32, (tq, tq), 1) <= lax.broadcasted_iota(I32, (tq, tq), 0)
    groups = lambda a: [a[:, g * LANES:(g + 1) * LANES] for g in range(ng)]

    for qi in range(nq):
        qs = qi * tq
        q = (q_ref[qs:qs + tq, :].astype(F32) * scale).astype(BF16)
        fq = jnp.sum(jnp.where(lane == fox_lane0 + h, fcol_ref[qs:qs + tq, :], 0.0), axis=-1, keepdims=True)
        fq_scr[...] = jnp.broadcast_to(fq, (tq, LANES))
        m_scr[...] = jnp.full((tq, LANES), NEG_BIG, F32)
        for kb in range(qi + 1):
            ks = kb * tq
            fk = frow_ref[pl.ds(fox_lane0 + h, 1), ks:ks + tq]
            s = lax.dot_general(q, k_ref[ks:ks + tq, :], (((1,), (1,)), ((), ())), preferred_element_type=F32)
            fqr = fq_scr[...]
            s = jnp.concatenate([sg + fqr for sg in groups(s)], axis=1) - fk
            if kb == qi:
                s = jnp.where(causal, s, NEG_BIG)
            s_scr[kb] = s
            m_scr[...] = functools.reduce(jnp.maximum, groups(s), m_scr[...])
        m_row = jnp.max(m_scr[...], axis=-1, keepdims=True)
        m_scr[...] = jnp.broadcast_to(m_row, (tq, LANES))
        l_scr[...] = jnp.zeros_like(l_scr)
        acc_scr[...] = jnp.zeros_like(acc_scr)
        for kb in range(qi + 1):
            ks = kb * tq
            mr = m_scr[...]
            ps = [jnp.exp(sg - mr) for sg in groups(s_scr[kb])]
            l_scr[...] += functools.reduce(jnp.add, ps)
            p = jnp.concatenate(ps, axis=1).astype(BF16)
            acc_scr[...] += jnp.dot(p, v_ref[ks:ks + tq, :], preferred_element_type=F32)
        l = jnp.sum(l_scr[...], axis=-1, keepdims=True)
        o_ref[qs:qs + tq, :] = (acc_scr[...] * (1.0 / l)).astype(o_ref.dtype)


def _fox(pmain, fcol, frow, *, batch, seq, n_heads, q_col, k_col, v_col, fox_lane0, tq):
    kern = functools.partial(_fox_kernel, tq=tq, fox_lane0=fox_lane0)
    head_block = lambda col0: pl.BlockSpec((seq, FOX_HEAD_DIM), lambda b, h: (b, col0 + h))
    return pl.pallas_call(
        kern,
        grid=(batch, n_heads),
        in_specs=[
            head_block(q_col), head_block(k_col), head_block(v_col),
            pl.BlockSpec((seq, LANES), lambda b, h: (b, 0)),
            pl.BlockSpec((None, LANES, seq), lambda b, h: (b, 0, 0)),
        ],
        out_specs=head_block(0),
        out_shape=jax.ShapeDtypeStruct((batch * seq, n_heads * FOX_HEAD_DIM), BF16),
        scratch_shapes=[
            pltpu.VMEM((seq // tq, tq, tq), F32),
            pltpu.VMEM((tq, LANES), F32),
            pltpu.VMEM((tq, LANES), F32),
            pltpu.VMEM((tq, LANES), F32),
            pltpu.VMEM((tq, FOX_HEAD_DIM), F32),
        ],
        compiler_params=_cparams(("parallel", "arbitrary")),
        name="fox_attn",
    )(pmain, pmain, pmain, fcol, frow)


def _xattn_kernel(q_ref, kv_ref, o_ref, *, d_model):
    hd = d_model // XA_HEADS
    scale = hd ** -0.5
    for h in range(XA_HEADS):
        q = q_ref[:, h * hd:(h + 1) * hd]
        k = kv_ref[:, h * hd:(h + 1) * hd]
        v = kv_ref[:, d_model + h * hd:d_model + (h + 1) * hd]
        s = lax.dot_general(q, k, (((1,), (1,)), ((), ())), preferred_element_type=F32) * scale
        p = jnp.exp(s - jnp.max(s, axis=-1, keepdims=True))
        p = p * (1.0 / jnp.sum(p, axis=-1, keepdims=True))
        o_ref[:, h * hd:(h + 1) * hd] = jnp.dot(p.astype(BF16), v, preferred_element_type=F32).astype(o_ref.dtype)


def _xattn(q, kv, *, batch, seq, mem_len, d_model, tq):
    nq = seq // tq
    return pl.pallas_call(
        functools.partial(_xattn_kernel, d_model=d_model),
        grid=(batch, nq),
        in_specs=[
            pl.BlockSpec((tq, d_model), lambda b, i: (b * nq + i, 0)),
            pl.BlockSpec((mem_len, 2 * d_model), lambda b, i: (b, 0)),
        ],
        out_specs=pl.BlockSpec((tq, d_model), lambda b, i: (b * nq + i, 0)),
        out_shape=jax.ShapeDtypeStruct((batch * seq, d_model), BF16),
        compiler_params=_cparams(("parallel", "arbitrary")),
        name="mem_xattn",
    )(q, kv)


def _router_kernel(h_ref, g_ref, wr_ref, br_ref, idx_ref, gate_ref, cnt_ref, *, n_experts):
    tm = h_ref.shape[0]

    @pl.when(pl.program_id(0) == 0)
    def _():
        cnt_ref[...] = jnp.zeros_like(cnt_ref)

    u = _rms(h_ref[...], g_ref[...])
    logits = jnp.dot(u, wr_ref[...], preferred_element_type=F32, precision=lax.Precision.HIGHEST) + br_ref[...]
    lane = lax.broadcasted_iota(I32, (tm, LANES), 1)
    work = jnp.where(lane < n_experts, logits, -jnp.inf)
    vals, idxs = [], []
    for _ in range(TOP_K):
        mx = jnp.max(work, axis=-1, keepdims=True)
        am = jnp.min(jnp.where(work == mx, lane, LANES), axis=-1, keepdims=True)
        vals.append(mx)
        idxs.append(am)
        work = jnp.where(lane == am, -jnp.inf, work)
    ex = [jnp.exp(v - vals[0]) for v in vals]
    den = ex[0] + ex[1] + ex[2] + ex[3]
    idx_out = jnp.zeros((tm, LANES), I32)
    gate_out = jnp.zeros((tm, LANES), F32)
    hits = jnp.zeros((tm, LANES), F32)
    for k in range(TOP_K):
        idx_out = jnp.where(lane == k, idxs[k], idx_out)
        gate_out = jnp.where(lane == k, ex[k] / den, gate_out)
        hits = hits + (lane == idxs[k]).astype(F32)
    idx_ref[...] = idx_out
    gate_ref[...] = gate_out
    cnt_ref[...] += jnp.broadcast_to(jnp.sum(hits, axis=0, keepdims=True), cnt_ref.shape)


def _router(h, g, wr, br, *, n_experts, tm):
    n, d = h.shape
    return pl.pallas_call(
        functools.partial(_router_kernel, n_experts=n_experts),
        grid=(n // tm,),
        in_specs=[
            pl.BlockSpec((tm, d), lambda i: (i, 0)),
            pl.BlockSpec((1, d), lambda i: (0, 0)),
            pl.BlockSpec((d, LANES), lambda i: (0, 0)),
            pl.BlockSpec((1, LANES), lambda i: (0, 0)),
        ],
        out_specs=[
            pl.BlockSpec((tm, LANES), lambda i: (i, 0)),
            pl.BlockSpec((tm, LANES), lambda i: (i, 0)),
            pl.BlockSpec((8, LANES), lambda i: (0, 0)),
        ],
        out_shape=[
            jax.ShapeDtypeStruct((n, LANES), I32),
            jax.ShapeDtypeStruct((n, LANES), F32),
            jax.ShapeDtypeStruct((8, LANES), F32),
        ],
        compiler_params=_cparams(("arbitrary",)),
        name="moe_router",
    )(h, g, wr, br)


def _slots_kernel(idx_ref, cnt_ref, dest_ref, be_ref, meta_ref, carry, *, n_experts, tm_e, nb_pad):
    t = idx_ref.shape[0]
    lane1 = lax.broadcasted_iota(I32, (1, LANES), 1)

    @pl.when(pl.program_id(0) == 0)
    def _():
        cnt = cnt_ref[0:1, :]
        padded = jnp.floor((cnt + (tm_e - 1)) / tm_e) * tm_e
        er = lax.broadcasted_iota(I32, (LANES, LANES), 0)
        ec = lax.broadcasted_iota(I32, (LANES, LANES), 1)
        start = _dot01_right(jnp.broadcast_to(padded, (8, LANES)), (er < ec).astype(BF16))[0:1, :]
        carry[...] = start
        pend = start + padded
        jrow = (lax.broadcasted_iota(I32, (nb_pad, LANES), 0) * tm_e).astype(F32)
        lane = lax.broadcasted_iota(I32, (nb_pad, LANES), 1)
        hit = jnp.where((pend <= jrow) & (lane < n_experts), 1.0, 0.0)
        be = jnp.minimum(jnp.sum(hit, axis=-1, keepdims=True), n_experts - 1.0)
        be_ref[...] = jnp.broadcast_to(be, (nb_pad, LANES)).astype(I32)
        total = jnp.sum(jnp.where(lane1 < n_experts, padded, 0.0), axis=-1, keepdims=True)
        meta = jnp.concatenate([start, cnt, pend, jnp.broadcast_to(total / tm_e, (1, LANES)),
                                jnp.zeros((4, LANES), F32)], axis=0)
        meta_ref[...] = meta.astype(I32)

    idx = idx_ref[...]
    lane = lax.broadcasted_iota(I32, (t, LANES), 1)
    ohs = [lane == idx[:, k:k + 1] for k in range(TOP_K)]
    oh = jnp.zeros((t, LANES), F32)
    for o in ohs:
        oh = oh + o.astype(F32)
    r = lax.broadcasted_iota(I32, (t, t), 0)
    c = lax.broadcasted_iota(I32, (t, t), 1)
    earlier = jnp.dot((c < r).astype(BF16), oh.astype(BF16), preferred_element_type=F32)
    base = carry[...] + earlier
    dest = jnp.zeros((t, LANES), F32)
    for k in range(TOP_K):
        dk = jnp.sum(jnp.where(ohs[k], base, 0.0), axis=-1, keepdims=True)
        dest = jnp.where(lane == k, dk, dest)
    dest_ref[...] = dest.astype(I32)
    carry[...] += jnp.sum(oh, axis=0, keepdims=True)


def _slots(idx, cnt, *, n_experts, tm_e, nb_pad, t):
    n = idx.shape[0]
    return pl.pallas_call(
        functools.partial(_slots_kernel, n_experts=n_experts, tm_e=tm_e, nb_pad=nb_pad),
        grid=(n // t,),
        in_specs=[pl.BlockSpec((t, LANES), lambda i: (i, 0)), pl.BlockSpec((8, LANES), lambda i: (0, 0))],
        out_specs=[
            pl.BlockSpec((t, LANES), lambda i: (i, 0)),
            pl.BlockSpec((nb_pad, LANES), lambda i: (0, 0)),
            pl.BlockSpec((8, LANES), lambda i: (0, 0)),
        ],
        out_shape=[
            jax.ShapeDtypeStruct((n, LANES), I32),
            jax.ShapeDtypeStruct((nb_pad, LANES), I32),
            jax.ShapeDtypeStruct((8, LANES), I32),
        ],
        scratch_shapes=[pltpu.VMEM((1, LANES), F32)],
        compiler_params=_cparams(("arbitrary",)),
        name="moe_slots",
    )(idx, cnt)


def _drain(copy, n):
    assert n % DRAIN_UNROLL == 0

    def body(_, c):
        for _ in range(DRAIN_UNROLL):
            copy.wait()
        return c

    lax.fori_loop(0, n // DRAIN_UNROLL, body, 0)


def _dispatch_kernel(dest_sm, fill_lo_sm, fill_hi_sm, h_ref, g_ref, buf_hbm, urows, zrow, sem, zsem, *, t, n_experts):
    i = pl.program_id(0)
    last = pl.num_programs(0) - 1
    cur = i % 2

    def row_copy(buf, j, slot):
        return pltpu.make_async_copy(urows.at[buf, pl.ds(j, 1)], buf_hbm.at[pl.ds(slot, 1)], sem.at[buf])

    def zero_copy(slot):
        return pltpu.make_async_copy(zrow.at[pl.ds(0, 1)], buf_hbm.at[pl.ds(slot, 1)], zsem)

    @pl.when(i == 0)
    def _():
        zrow[...] = jnp.zeros_like(zrow)
        for e in range(n_experts):
            lo, hi = fill_lo_sm[e], fill_hi_sm[e]

            def zstart(s, c):
                zero_copy(s).start()
                return c

            def zwait(s, c):
                zero_copy(s).wait()
                return c

            lax.fori_loop(lo, hi, zstart, 0)
            lax.fori_loop(lo, hi, zwait, 0)

        def tail_copy(b):
            s = pl.multiple_of(fill_hi_sm[n_experts - 1] + b * zrow.shape[0], zrow.shape[0])
            return pltpu.make_async_copy(zrow, buf_hbm.at[pl.ds(s, zrow.shape[0])], zsem)

        n_tail = (buf_hbm.shape[0] - fill_hi_sm[n_experts - 1]) // zrow.shape[0]

        def tstart(b, c):
            tail_copy(b).start()
            return c

        def twait(b, c):
            tail_copy(b).wait()
            return c

        lax.fori_loop(0, n_tail, tstart, 0)
        lax.fori_loop(0, n_tail, twait, 0)

    base = i * t
    urows[cur] = _pack_bf16_pairs(_rms(h_ref[...], g_ref[...]))

    def issue(jb, c):
        j0 = pl.multiple_of(jb * ISSUE_GROUP, ISSUE_GROUP)
        for r in range(ISSUE_GROUP):
            for k in range(TOP_K):
                row_copy(cur, j0 + r, dest_sm[(base + j0 + r) * TOP_K + k]).start(priority=k % 2)
        return c

    lax.fori_loop(0, t // ISSUE_GROUP, issue, 0)

    @pl.when(i > 0)
    def _():
        _drain(row_copy(1 - cur, 0, 0), t * TOP_K)

    @pl.when(i == last)
    def _():
        _drain(row_copy(cur, 0, 0), t * TOP_K)


def _dispatch(dest_flat, fill_lo, fill_hi, h, g, *, n_slots, n_experts, t):
    n, d = h.shape
    return pl.pallas_call(
        functools.partial(_dispatch_kernel, t=t, n_experts=n_experts),
        grid_spec=pltpu.PrefetchScalarGridSpec(
            num_scalar_prefetch=3,
            grid=(n // t,),
            in_specs=[pl.BlockSpec((t, d), lambda i, *_: (i, 0)), pl.BlockSpec((1, d), lambda i, *_: (0, 0))],
            out_specs=pl.BlockSpec(memory_space=pl.ANY),
            scratch_shapes=[pltpu.VMEM((2, t, d // 2), U32), pltpu.VMEM((8, d // 2), U32),
                            pltpu.SemaphoreType.DMA((2,)), pltpu.SemaphoreType.DMA(())],
        ),
        out_shape=jax.ShapeDtypeStruct((n_slots, d // 2), U32),
        compiler_params=_cparams(("arbitrary",)),
        name="moe_dispatch",
    )(dest_flat, fill_lo, fill_hi, h, g)


def _pack_bf16_pairs(x):
    w = x.shape[1] // 2
    bits = lambda a: lax.bitcast_convert_type(a.astype(BF16).astype(F32), U32)
    return (bits(x[:, :w]) >> 16) | (bits(x[:, w:]) & jnp.uint32(0xFFFF0000))


def _unpack_bf16_pairs(p):
    lo = lax.bitcast_convert_type(p << 16, F32)
    hi = lax.bitcast_convert_type(p & jnp.uint32(0xFFFF0000), F32)
    return lo, hi


def _expert_kernel(be_sm, nu_sm, x_ref, w1g_ref, w1u_ref, b1g_ref, b1u_ref, w2_ref, b2_ref, o_ref,
                   xb, acc, w1g_res, w1u_res, w2_res):
    j = pl.program_id(0)
    c = pl.program_id(1)
    nc = pl.num_programs(1)
    half = xb.shape[1] // 2

    @pl.when(j < nu_sm[0])
    def _():
        @pl.when(_first_block_of_expert(j, be_sm))
        def _():
            w1g_res[c] = w1g_ref[...].astype(BF16)
            w1u_res[c] = w1u_ref[...].astype(BF16)
            w2_res[c] = w2_ref[...].astype(BF16)

        @pl.when(c == 0)
        def _():
            lo, hi = _unpack_bf16_pairs(x_ref[...])
            xb[:, :half] = lo.astype(BF16)
            xb[:, half:] = hi.astype(BF16)
            acc[...] = jnp.broadcast_to(b2_ref[...], acc.shape)

        x = xb[...]
        g = jnp.dot(x, w1g_res[c], preferred_element_type=F32) + b1g_ref[...]
        u = jnp.dot(x, w1u_res[c], preferred_element_type=F32) + b1u_ref[...]
        g = jnp.minimum(g, SWIGLU_LIMIT)
        u = jnp.clip(u, -SWIGLU_LIMIT, SWIGLU_LIMIT)
        act = g * jax.nn.sigmoid(SWIGLU_ALPHA * g) * (u + 1.0)
        acc[...] += jnp.dot(act.astype(BF16), w2_res[c], preferred_element_type=F32)

        @pl.when(c == nc - 1)
        def _():
            o_ref[...] = _pack_bf16_pairs(acc[...])

    @pl.when((j >= nu_sm[0]) & (c == nc - 1))
    def _():
        o_ref[...] = jnp.zeros_like(o_ref)


def _first_block_of_expert(j, be_sm):
    return (j == 0) | (be_sm[j] != be_sm[jnp.maximum(j - 1, 0)])


def _experts(blk_expert, n_used, buf, w1, b1, w2, b2, *, tm, tc):
    n_slots, half = buf.shape
    d = 2 * half
    ne, _, two_de = w1.shape
    de = two_de // 2
    nc = de // tc
    nb = n_slots // tm

    def blk(j, nu):
        return jnp.minimum(j, nu[0] - 1)

    def wchunk(j, c, be, nu):
        stream = (j < nu[0]) & _first_block_of_expert(j, be)
        return jnp.where(stream, c, nc - 1)

    def bchunk(j, c, nu):
        return jnp.where(j < nu[0], c, nc - 1)

    return pl.pallas_call(
        _expert_kernel,
        grid_spec=pltpu.PrefetchScalarGridSpec(
            num_scalar_prefetch=2,
            grid=(nb, nc),
            in_specs=[
                pl.BlockSpec((tm, half), lambda j, c, be, nu: (blk(j, nu), 0)),
                pl.BlockSpec((None, d, tc), lambda j, c, be, nu: (be[blk(j, nu)], 0, wchunk(j, c, be, nu))),
                pl.BlockSpec((None, d, tc), lambda j, c, be, nu: (be[blk(j, nu)], 0, nc + wchunk(j, c, be, nu))),
                pl.BlockSpec((None, 1, tc), lambda j, c, be, nu: (be[blk(j, nu)], 0, bchunk(j, c, nu))),
                pl.BlockSpec((None, 1, tc), lambda j, c, be, nu: (be[blk(j, nu)], 0, nc + bchunk(j, c, nu))),
                pl.BlockSpec((None, tc, d), lambda j, c, be, nu: (be[blk(j, nu)], wchunk(j, c, be, nu), 0)),
                pl.BlockSpec((None, 1, d), lambda j, c, be, nu: (be[blk(j, nu)], 0, 0)),
            ],
            out_specs=pl.BlockSpec((tm, half), lambda j, c, be, nu: (j, 0)),
            scratch_shapes=[
                pltpu.VMEM((tm, d), BF16), pltpu.VMEM((tm, d), F32),
                pltpu.VMEM((nc, d, tc), BF16), pltpu.VMEM((nc, d, tc), BF16), pltpu.VMEM((nc, tc, d), BF16),
            ],
        ),
        out_shape=jax.ShapeDtypeStruct((n_slots, half), U32),
        compiler_params=_cparams(("arbitrary", "arbitrary")),
        name="moe_experts",
    )(blk_expert, n_used, buf, w1, w1, b1, b1, w2, b2)


def _combine_kernel(dest_sm, gate_ref, h_ref, g_ref, y_hbm, o_ref, rows, sem, *, t):
    i = pl.program_id(0)
    last = pl.num_programs(0) - 1
    cur = i % 2

    def row_copy(buf, slot, k, j):
        return pltpu.make_async_copy(y_hbm.at[pl.ds(slot, 1)], rows.at[buf, k, pl.ds(j, 1)], sem.at[buf])

    def gather(tile, buf):
        def issue(jb, c):
            j0 = pl.multiple_of(jb * ISSUE_GROUP, ISSUE_GROUP)
            for r in range(ISSUE_GROUP):
                for k in range(TOP_K):
                    row_copy(buf, dest_sm[(tile * t + j0 + r) * TOP_K + k], k, j0 + r).start(priority=k % 2)
            return c

        lax.fori_loop(0, t // ISSUE_GROUP, issue, 0)

    @pl.when(i == 0)
    def _():
        gather(0, 0)

    @pl.when(i < last)
    def _():
        gather(i + 1, 1 - cur)

    _drain(row_copy(cur, 0, 0, 0), t * TOP_K)
    gate = gate_ref[...]
    half = h_ref.shape[1] // 2
    y_lo = h_ref[:, :half]
    y_hi = h_ref[:, half:]
    for k in range(TOP_K):
        lo, hi = _unpack_bf16_pairs(rows[cur, k])
        y_lo = y_lo + gate[:, k:k + 1] * lo
        y_hi = y_hi + gate[:, k:k + 1] * hi
    o_ref[...] = _rms(jnp.concatenate([y_lo, y_hi], axis=1), g_ref[...])


def _combine(dest_flat, gate, h, g, ybuf, *, t):
    n, d = h.shape
    return pl.pallas_call(
        functools.partial(_combine_kernel, t=t),
        grid_spec=pltpu.PrefetchScalarGridSpec(
            num_scalar_prefetch=1,
            grid=(n // t,),
            in_specs=[
                pl.BlockSpec((t, LANES), lambda i, ds: (i, 0)),
                pl.BlockSpec((t, d), lambda i, ds: (i, 0)),
                pl.BlockSpec((1, d), lambda i, ds: (0, 0)),
                pl.BlockSpec(memory_space=pl.ANY),
            ],
            out_specs=pl.BlockSpec((t, d), lambda i, ds: (i, 0)),
            scratch_shapes=[pltpu.VMEM((2, TOP_K, t, d // 2), U32), pltpu.SemaphoreType.DMA((2,))],
        ),
        out_shape=jax.ShapeDtypeStruct((n, d), F32),
        compiler_params=_cparams(("arbitrary",)),
        name="moe_combine",
    )(dest_flat, gate, h, g, ybuf)


def _pad_lanes(v, lane0=0):
    out = jnp.zeros((1, LANES), F32)
    return out.at[0, lane0:lane0 + v.shape[0]].set(v.astype(F32))


def kernel(x, mem, ln_mix, w_in, conv_w, conv_b, dt_bias, a_log, d_skip, ssd_norm, fgate_bias, w_out, ln_xa, ln_mem,
           w_xq, w_xkv, w_xo, ln_ffn, w_router, b_router, w_moe1, b_moe1, w_moe2, b_moe2, ln_final):
    batch, seq, d = x.shape
    mem_len = mem.shape[1]
    depth = ln_mix.shape[0]
    n_ssd_heads = dt_bias.shape[1]
    n_fox_heads = fgate_bias.shape[1]
    d_ssd = n_ssd_heads * SSD_HEAD_DIM
    d_fox = n_fox_heads * FOX_HEAD_DIM
    cdim = d_ssd + 2 * SSD_GROUPS * SSD_STATE
    n_experts = w_router.shape[2]
    n = batch * seq
    fox_lane0 = n_ssd_heads
    assert n_ssd_heads + n_fox_heads <= LANES and cdim % d_ssd == 0 and d_ssd == d and d_fox == d

    tm = min(1024, n)
    o_z, o_xbc, o_dt = 0, d_ssd, d_ssd + cdim
    o_q = o_dt + n_ssd_heads
    o_k, o_v, o_f = o_q + d_fox, o_q + 2 * d_fox, o_q + 3 * d_fox
    q_col = (cdim + d_ssd) // FOX_HEAD_DIM
    k_col = q_col + n_fox_heads
    v_col = k_col + n_fox_heads

    head_of_ch = jnp.arange(d_ssd, dtype=I32) // SSD_HEAD_DIM
    e01 = (jnp.arange(LANES, dtype=I32)[:, None] == head_of_ch[None, :]).astype(BF16)

    n_assign = n * TOP_K
    nb = -(-n_assign // MOE_TM) + n_experts
    nb_pad = -(-nb // 8) * 8
    n_slots = nb * MOE_TM

    h = x.reshape(n, d)
    mem2 = mem.reshape(batch * mem_len, d)
    for i in range(depth):
        w = w_in[i]
        w_main = jnp.concatenate([w[:, o_xbc:o_dt], w[:, o_z:o_xbc], w[:, o_q:o_f]], axis=1).astype(BF16)
        w_small = jnp.concatenate(
            [w[:, o_dt:o_q], w[:, o_f:], jnp.zeros((d, LANES - n_ssd_heads - n_fox_heads), F32)], axis=1).astype(BF16)
        pmain, psmall = _inproj(h, ln_mix[i][None, :], w_main, w_small, tm, 1024)

        y_ssd, fcol, frow = _ssd(
            pmain, psmall, conv_w[i], conv_b[i][None, :], _pad_lanes(dt_bias[i]), _pad_lanes(a_log[i]),
            _pad_lanes(fgate_bias[i], fox_lane0), jnp.repeat(d_skip[i].astype(F32), SSD_HEAD_DIM)[None, :],
            ssd_norm[i][None, :], e01, batch=batch, seq=seq, d_inner=d_ssd, n_heads=n_ssd_heads, fox_lane0=fox_lane0)
        y_fox = _fox(pmain, fcol, frow, batch=batch, seq=seq, n_heads=n_fox_heads, q_col=q_col, k_col=k_col,
                     v_col=v_col, fox_lane0=fox_lane0, tq=min(256, seq))
        h = _mm_res([y_ssd, y_fox], w_out[i].astype(BF16), h, tm, 512, "out_proj")

        q = _norm_matmul(h, ln_xa[i][None, :], w_xq[i].astype(BF16), tm, 1024, "xa_q")
        kv = _norm_matmul(mem2, ln_mem[i][None, :], w_xkv[i].astype(BF16), min(1024, batch * mem_len), 1024, "xa_kv")
        o = _xattn(q, kv, batch=batch, seq=seq, mem_len=mem_len, d_model=d, tq=min(512, seq))
        h = _mm_res([o], w_xo[i].astype(BF16), h, tm, 512, "xa_o")

        wr = jnp.concatenate([w_router[i], jnp.zeros((d, LANES - n_experts), F32)], axis=1)
        idx, gate, cnt = _router(h, ln_ffn[i][None, :], wr, _pad_lanes(b_router[i]), n_experts=n_experts,
                                 tm=min(512, n))
        dest, be, meta = _slots(idx, cnt, n_experts=n_experts, tm_e=MOE_TM, nb_pad=nb_pad, t=min(512, n))
        dest_flat = dest[:, :TOP_K].reshape(-1)
        fill_lo = meta[0, :n_experts] + meta[1, :n_experts]
        fill_hi = meta[2, :n_experts]
        buf = _dispatch(dest_flat, fill_lo, fill_hi, h, ln_ffn[i][None, :], n_slots=n_slots, n_experts=n_experts,
                        t=min(256, n))
        ybuf = _experts(be[:nb, 0], meta[3, :1], buf, w_moe1[i], b_moe1[i][:, None, :], w_moe2[i],
                        b_moe2[i][:, None, :], tm=MOE_TM, tc=MOE_TC)
        last = i == depth - 1
        assert last, "the fused final norm assumes a single layer"
        h = _combine(dest_flat, gate, h, ln_final[None, :], ybuf, t=min(256, n))
    return h.reshape(batch, seq, d)
```

```python
import functools

import jax
import jax.numpy as jnp
from jax import lax
from jax.experimental import pallas as pl
from jax.experimental.pallas import tpu as pltpu

F32 = jnp.float32
BF16 = jnp.bfloat16
I32 = jnp.int32
U32 = jnp.uint32

RMS_EPS = 1e-5
LANES = 128
SSD_HEAD_DIM = 64
SSD_GROUPS = 8
SSD_STATE = 128
SSD_CONV = 4
SSD_CHUNK = 128
CONV_HALO = 8
FOX_HEAD_DIM = 128
XA_HEADS = 4
TOP_K = 4
SWIGLU_LIMIT = 7.0
SWIGLU_ALPHA = 1.702
NEG_BIG = -1e30
VMEM_LIMIT = 56 * 1024 * 1024

MOE_TM = 512
MOE_TC = 256
DRAIN_UNROLL = 32
ISSUE_GROUP = 8


def _cparams(sem):
    return pltpu.CompilerParams(dimension_semantics=sem, vmem_limit_bytes=VMEM_LIMIT)


def _rms(x, g):
    ms = jnp.mean(x * x, axis=-1, keepdims=True)
    return x * lax.rsqrt(ms + RMS_EPS) * g


def _softplus(x):
    return jnp.maximum(x, 0.0) + jnp.log1p(jnp.exp(-jnp.abs(x)))


def _split3(x):
    hi = x.astype(BF16)
    r1 = x - hi.astype(F32)
    mid = r1.astype(BF16)
    lo = (r1 - mid.astype(F32)).astype(BF16)
    return hi, mid, lo


def _dot01_right(x, m01):
    hi, mid, lo = _split3(x)
    d = lambda a: jnp.dot(a, m01, preferred_element_type=F32)
    return d(hi) + d(mid) + d(lo)


def _dot01_left(m01, x):
    hi, mid, lo = _split3(x)
    d = lambda a: jnp.dot(m01, a, preferred_element_type=F32)
    return d(hi) + d(mid) + d(lo)


def _inproj_kernel(x_ref, g_ref, w_ref, ws_ref, o_ref, os_ref, u_scr):
    @pl.when(pl.program_id(1) == 0)
    def _():
        u = _rms(x_ref[...], g_ref[...]).astype(BF16)
        u_scr[...] = u
        os_ref[...] = jnp.dot(u, ws_ref[...], preferred_element_type=F32)

    o_ref[...] = jnp.dot(u_scr[...], w_ref[...], preferred_element_type=F32).astype(o_ref.dtype)


def _inproj(x, g, w_main, w_small, tm, tn):
    m, k = x.shape
    n = w_main.shape[1]
    return pl.pallas_call(
        _inproj_kernel,
        grid=(m // tm, n // tn),
        in_specs=[
            pl.BlockSpec((tm, k), lambda i, j: (i, 0)),
            pl.BlockSpec((1, k), lambda i, j: (0, 0)),
            pl.BlockSpec((k, tn), lambda i, j: (0, j)),
            pl.BlockSpec((k, LANES), lambda i, j: (0, 0)),
        ],
        out_specs=[
            pl.BlockSpec((tm, tn), lambda i, j: (i, j)),
            pl.BlockSpec((tm, LANES), lambda i, j: (i, 0)),
        ],
        out_shape=[jax.ShapeDtypeStruct((m, n), BF16), jax.ShapeDtypeStruct((m, LANES), F32)],
        scratch_shapes=[pltpu.VMEM((tm, k), BF16)],
        compiler_params=_cparams(("parallel", "arbitrary")),
        name="inproj",
    )(x, g, w_main, w_small)


def _norm_matmul_kernel(x_ref, g_ref, w_ref, o_ref, u_scr):
    @pl.when(pl.program_id(1) == 0)
    def _():
        u_scr[...] = _rms(x_ref[...], g_ref[...]).astype(BF16)

    o_ref[...] = jnp.dot(u_scr[...], w_ref[...], preferred_element_type=F32).astype(o_ref.dtype)


def _norm_matmul(x, g, w, tm, tn, name):
    m, k = x.shape
    n = w.shape[1]
    return pl.pallas_call(
        _norm_matmul_kernel,
        grid=(m // tm, n // tn),
        in_specs=[
            pl.BlockSpec((tm, k), lambda i, j: (i, 0)),
            pl.BlockSpec((1, k), lambda i, j: (0, 0)),
            pl.BlockSpec((k, tn), lambda i, j: (0, j)),
        ],
        out_specs=pl.BlockSpec((tm, tn), lambda i, j: (i, j)),
        out_shape=jax.ShapeDtypeStruct((m, n), BF16),
        scratch_shapes=[pltpu.VMEM((tm, k), BF16)],
        compiler_params=_cparams(("parallel", "arbitrary")),
        name=name,
    )(x, g, w)


def _mm_res_kernel(*refs, n_lhs):
    res_ref, o_ref = refs[2 * n_lhs], refs[2 * n_lhs + 1]
    acc = res_ref[...]
    for a_ref, w_ref in zip(refs[:n_lhs], refs[n_lhs:2 * n_lhs]):
        acc = acc + jnp.dot(a_ref[...], w_ref[...], preferred_element_type=F32)
    o_ref[...] = acc


def _mm_res(lhs, w, res, tm, tn, name):
    m, n = res.shape
    n_lhs = len(lhs)
    kb = lhs[0].shape[1]
    assert all(a.shape[1] == kb for a in lhs) and w.shape[0] == n_lhs * kb
    ws = [w] * n_lhs
    in_specs = [pl.BlockSpec((tm, kb), lambda i, j: (i, 0)) for _ in lhs]
    in_specs += [pl.BlockSpec((kb, tn), lambda i, j, r=r: (r, j)) for r in range(n_lhs)]
    in_specs += [pl.BlockSpec((tm, tn), lambda i, j: (i, j))]
    return pl.pallas_call(
        functools.partial(_mm_res_kernel, n_lhs=n_lhs),
        grid=(m // tm, n // tn),
        in_specs=in_specs,
        out_specs=pl.BlockSpec((tm, tn), lambda i, j: (i, j)),
        out_shape=jax.ShapeDtypeStruct((m, n), F32),
        compiler_params=_cparams(("parallel", "arbitrary")),
        name=name,
    )(*lhs, *ws, res)


def _ssd_kernel(xbc_ref, z_ref, sm_ref, cw_ref, cb_ref, dtb_ref, alog_ref, fb_ref, dch_ref, gn_ref, e_ref,
                y_ref, fcol_ref, frow_ref,
                ext, cv, state, fcarry, yacc, dtch, csch, *, d_inner, n_heads, fox_lane0):
    L = SSD_CHUNK
    gw = d_inner // SSD_GROUPS
    hpg = gw // SSD_HEAD_DIM
    cdim = d_inner + 2 * SSD_GROUPS * SSD_STATE
    strip = 256

    @pl.when(pl.program_id(1) == 0)
    def _():
        ext[...] = jnp.zeros_like(ext)
        state[...] = jnp.zeros_like(state)
        fcarry[...] = jnp.zeros_like(fcarry)

    row = lax.broadcasted_iota(I32, (L, L), 0)
    col = lax.broadcasted_iota(I32, (L, L), 1)
    causal = col <= row
    tri = causal.astype(BF16)

    shifts = [(col == row - (SSD_CONV - 1 - k)).astype(BF16) for k in range(SSD_CONV - 1)]
    for s0 in range(0, cdim, strip):
        cur = xbc_ref[:, s0:s0 + strip]
        acc = cb_ref[:, s0:s0 + strip] + cw_ref[SSD_CONV - 1:SSD_CONV, s0:s0 + strip] * cur.astype(F32)
        head = jnp.zeros((CONV_HALO, strip), F32)
        for k in range(SSD_CONV - 1):
            wk = cw_ref[k:k + 1, s0:s0 + strip]
            acc = acc + wk * jnp.dot(shifts[k], cur, preferred_element_type=F32)
            r0 = CONV_HALO - (SSD_CONV - 1) + k
            head = head + wk * ext[r0:r0 + CONV_HALO, s0:s0 + strip]
        acc = jnp.concatenate([acc[0:CONV_HALO] + head, acc[CONV_HALO:]], axis=0)
        cv[:, s0:s0 + strip] = acc * jax.nn.sigmoid(acc)
    ext[0:CONV_HALO, :] = xbc_ref[L - CONV_HALO:L, :].astype(F32)

    sm = sm_ref[...]
    dt = _softplus(sm + dtb_ref[...])
    dta = dt * (-jnp.exp(alog_ref[...]))
    cs = _dot01_left(tri, dta)
    cst = cs.T
    logf = -_softplus(-(sm + fb_ref[...]))
    fc = _dot01_left(tri, logf) + fcarry[...]
    fcol_ref[...] = fc
    frow_ref[...] = fc.T
    fcarry[...] = fc[L - 1:L, :]

    e01 = e_ref[...]
    dtch[...] = _dot01_right(dt, e01)
    csch[...] = _dot01_right(cs, e01)

    lane_g = lax.broadcasted_iota(I32, (1, gw), 1)
    for g in range(SSD_GROUPS):
        c0 = g * gw
        bg = cv[:, d_inner + g * SSD_STATE:d_inner + (g + 1) * SSD_STATE]
        cg = cv[:, d_inner + SSD_GROUPS * SSD_STATE + g * SSD_STATE:d_inner + SSD_GROUPS * SSD_STATE + (g + 1) * SSD_STATE]
        cgb = cg.astype(BF16)
        cbm = lax.dot_general(cgb, bg.astype(BF16), (((1,), (1,)), ((), ())), preferred_element_type=F32)
        bgt = bg.T.astype(BF16)
        xg = cv[:, c0:c0 + gw]
        csg = csch[:, c0:c0 + gw]
        cs_last = csg[L - 1:L, :]
        xdt = xg * dtch[:, c0:c0 + gw]
        xw = (xdt * jnp.exp(cs_last - csg)).astype(BF16)
        st_new = jnp.dot(bgt, xw, preferred_element_type=F32)
        prev = state[g]
        yg = jnp.dot(cgb, prev.astype(BF16), preferred_element_type=F32) * jnp.exp(csg)
        state[g] = prev * jnp.exp(cs_last) + st_new
        yg = yg + xg * dch_ref[:, c0:c0 + gw]
        for r in range(hpg):
            h = g * hpg + r
            seg = cs[:, h:h + 1] - cst[h:h + 1, :]
            m = (cbm * jnp.exp(jnp.where(causal, seg, NEG_BIG))).astype(BF16)
            head = (lane_g >= r * SSD_HEAD_DIM) & (lane_g < (r + 1) * SSD_HEAD_DIM)
            xh = jnp.where(head, xdt, 0.0).astype(BF16)
            yg = yg + jnp.dot(m, xh, preferred_element_type=F32)
        yacc[:, c0:c0 + gw] = yg

    z = z_ref[...].astype(F32)
    gated = yacc[...] * (z * jax.nn.sigmoid(z))
    y_ref[...] = _rms(gated, gn_ref[...]).astype(y_ref.dtype)


def _ssd(pmain, psmall, conv_w, conv_b, dtb, alog, fb, dch, gnorm, e01, *, batch, seq, d_inner, n_heads, fox_lane0):
    nc = seq // SSD_CHUNK
    cdim = d_inner + 2 * SSD_GROUPS * SSD_STATE
    gw = d_inner // SSD_GROUPS
    m = batch * seq
    rowblk = lambda b, c: b * nc + c
    const = lambda shape: pl.BlockSpec(shape, lambda b, c: (0, 0))
    kern = functools.partial(_ssd_kernel, d_inner=d_inner, n_heads=n_heads, fox_lane0=fox_lane0)
    return pl.pallas_call(
        kern,
        grid=(batch, nc),
        in_specs=[
            pl.BlockSpec((SSD_CHUNK, cdim), lambda b, c: (rowblk(b, c), 0)),
            pl.BlockSpec((SSD_CHUNK, d_inner), lambda b, c: (rowblk(b, c), cdim // d_inner)),
            pl.BlockSpec((SSD_CHUNK, LANES), lambda b, c: (rowblk(b, c), 0)),
            const((SSD_CONV, cdim)), const((1, cdim)), const((1, LANES)), const((1, LANES)), const((1, LANES)),
            const((1, d_inner)), const((1, d_inner)), const((LANES, d_inner)),
        ],
        out_specs=[
            pl.BlockSpec((SSD_CHUNK, d_inner), lambda b, c: (rowblk(b, c), 0)),
            pl.BlockSpec((SSD_CHUNK, LANES), lambda b, c: (rowblk(b, c), 0)),
            pl.BlockSpec((None, LANES, SSD_CHUNK), lambda b, c: (b, 0, c)),
        ],
        out_shape=[
            jax.ShapeDtypeStruct((m, d_inner), BF16),
            jax.ShapeDtypeStruct((m, LANES), F32),
            jax.ShapeDtypeStruct((batch, LANES, seq), F32),
        ],
        scratch_shapes=[
            pltpu.VMEM((2 * CONV_HALO, cdim), F32),
            pltpu.VMEM((SSD_CHUNK, cdim), F32),
            pltpu.VMEM((SSD_GROUPS, SSD_STATE, gw), F32),
            pltpu.VMEM((1, LANES), F32),
            pltpu.VMEM((SSD_CHUNK, d_inner), F32),
            pltpu.VMEM((SSD_CHUNK, d_inner), F32),
            pltpu.VMEM((SSD_CHUNK, d_inner), F32),
        ],
        compiler_params=_cparams(("arbitrary", "arbitrary")),
        name="ssd_scan",
    )(pmain, pmain, psmall, conv_w, conv_b, dtb, alog, fb, dch, gnorm, e01)


def _fox_kernel(q_ref, k_ref, v_ref, fcol_ref, frow_ref, o_ref, s_scr, fq_scr, m_scr, l_scr, acc_scr, *, tq, fox_lane0):
    h = pl.program_id(1)
    nq = q_ref.shape[0] // tq
    ng = tq // LANES
    scale = FOX_HEAD_DIM ** -0.5
    lane = lax.broadcasted_iota(I32, (tq, LANES), 1)
    causal = lax.broadcasted_iota(I32, (tq, tq), 1) <= lax.broadcasted_iota(I32, (tq, tq), 0)
    groups = lambda a: [a[:, g * LANES:(g + 1) * LANES] for g in range(ng)]

    for qi in range(nq):
        qs = qi * tq
        q = (q_ref[qs:qs + tq, :].astype(F32) * scale).astype(BF16)
        fq = jnp.sum(jnp.where(lane == fox_lane0 + h, fcol_ref[qs:qs + tq, :], 0.0), axis=-1, keepdims=True)
        fq_scr[...] = jnp.broadcast_to(fq, (tq, LANES))
        m_scr[...] = jnp.full((tq, LANES), NEG_BIG, F32)
        for kb in range(qi + 1):
            ks = kb * tq
            fk = frow_ref[pl.ds(fox_lane0 + h, 1), ks:ks + tq]
            s = lax.dot_general(q, k_ref[ks:ks + tq, :], (((1,), (1,)), ((), ())), preferred_element_type=F32)
            fqr = fq_scr[...]
            s = jnp.concatenate([sg + fqr for sg in groups(s)], axis=1) - fk
            if kb == qi:
                s = jnp.where(causal, s, NEG_BIG)
            s_scr[kb] = s
            m_scr[...] = functools.reduce(jnp.maximum, groups(s), m_scr[...])
        m_row = jnp.max(m_scr[...], axis=-1, keepdims=True)
        m_scr[...] = jnp.broadcast_to(m_row, (tq, LANES))
        l_scr[...] = jnp.zeros_like(l_scr)
        acc_scr[...] = jnp.zeros_like(acc_scr)
        for kb in range(qi + 1):
            ks = kb * tq
            mr = m_scr[...]
            ps = [jnp.exp(sg - mr) for sg in groups(s_scr[kb])]
            l_scr[...] += functools.reduce(jnp.add, ps)
            p = jnp.concatenate(ps, axis=1).astype(BF16)
            acc_scr[...] += jnp.dot(p, v_ref[ks:ks + tq, :], preferred_element_type=F32)
        l = jnp.sum(l_scr[...], axis=-1, keepdims=True)
        o_ref[qs:qs + tq, :] = (acc_scr[...] * (1.0 / l)).astype(o_ref.dtype)


def _fox(pmain, fcol, frow, *, batch, seq, n_heads, q_col, k_col, v_col, fox_lane0, tq):
    kern = functools.partial(_fox_kernel, tq=tq, fox_lane0=fox_lane0)
    head_block = lambda col0: pl.BlockSpec((seq, FOX_HEAD_DIM), lambda b, h: (b, col0 + h))
    return pl.pallas_call(
        kern,
        grid=(batch, n_heads),
        in_specs=[
            head_block(q_col), head_block(k_col), head_block(v_col),
            pl.BlockSpec((seq, LANES), lambda b, h: (b, 0)),
            pl.BlockSpec((None, LANES, seq), lambda b, h: (b, 0, 0)),
        ],
        out_specs=head_block(0),
        out_shape=jax.ShapeDtypeStruct((batch * seq, n_heads * FOX_HEAD_DIM), BF16),
        scratch_shapes=[
            pltpu.VMEM((seq // tq, tq, tq), F32),
            pltpu.VMEM((tq, LANES), F32),
            pltpu.VMEM((tq, LANES), F32),
            pltpu.VMEM((tq, LANES), F32),
            pltpu.VMEM((tq, FOX_HEAD_DIM), F32),
        ],
        compiler_params=_cparams(("parallel", "arbitrary")),
        name="fox_attn",
    )(pmain, pmain, pmain, fcol, frow)


def _xattn_kernel(q_ref, kv_ref, o_ref, *, d_model):
    hd = d_model // XA_HEADS
    scale = hd ** -0.5
    for h in range(XA_HEADS):
        q = q_ref[:, h * hd:(h + 1) * hd]
        k = kv_ref[:, h * hd:(h + 1) * hd]
        v = kv_ref[:, d_model + h * hd:d_model + (h + 1) * hd]
        s = lax.dot_general(q, k, (((1,), (1,)), ((), ())), preferred_element_type=F32) * scale
        p = jnp.exp(s - jnp.max(s, axis=-1, keepdims=True))
        p = p * (1.0 / jnp.sum(p, axis=-1, keepdims=True))
        o_ref[:, h * hd:(h + 1) * hd] = jnp.dot(p.astype(BF16), v, preferred_element_type=F32).astype(o_ref.dtype)


def _xattn(q, kv, *, batch, seq, mem_len, d_model, tq):
    nq = seq // tq
    return pl.pallas_call(
        functools.partial(_xattn_kernel, d_model=d_model),
        grid=(batch, nq),
        in_specs=[
            pl.BlockSpec((tq, d_model), lambda b, i: (b * nq + i, 0)),
            pl.BlockSpec((mem_len, 2 * d_model), lambda b, i: (b, 0)),
        ],
        out_specs=pl.BlockSpec((tq, d_model), lambda b, i: (b * nq + i, 0)),
        out_shape=jax.ShapeDtypeStruct((batch * seq, d_model), BF16),
        compiler_params=_cparams(("parallel", "arbitrary")),
        name="mem_xattn",
    )(q, kv)


def _router_kernel(h_ref, g_ref, wr_ref, br_ref, idx_ref, gate_ref, cnt_ref, *, n_experts):
    tm = h_ref.shape[0]

    @pl.when(pl.program_id(0) == 0)
    def _():
        cnt_ref[...] = jnp.zeros_like(cnt_ref)

    u = _rms(h_ref[...], g_ref[...])
    logits = jnp.dot(u, wr_ref[...], preferred_element_type=F32, precision=lax.Precision.HIGHEST) + br_ref[...]
    lane = lax.broadcasted_iota(I32, (tm, LANES), 1)
    work = jnp.where(lane < n_experts, logits, -jnp.inf)
    vals, idxs = [], []
    for _ in range(TOP_K):
        mx = jnp.max(work, axis=-1, keepdims=True)
        am = jnp.min(jnp.where(work == mx, lane, LANES), axis=-1, keepdims=True)
        vals.append(mx)
        idxs.append(am)
        work = jnp.where(lane == am, -jnp.inf, work)
    ex = [jnp.exp(v - vals[0]) for v in vals]
    den = ex[0] + ex[1] + ex[2] + ex[3]
    idx_out = jnp.zeros((tm, LANES), I32)
    gate_out = jnp.zeros((tm, LANES), F32)
    hits = jnp.zeros((tm, LANES), F32)
    for k in range(TOP_K):
        idx_out = jnp.where(lane == k, idxs[k], idx_out)
        gate_out = jnp.where(lane == k, ex[k] / den, gate_out)
        hits = hits + (lane == idxs[k]).astype(F32)
    idx_ref[...] = idx_out
    gate_ref[...] = gate_out
    cnt_ref[...] += jnp.broadcast_to(jnp.sum(hits, axis=0, keepdims=True), cnt_ref.shape)


def _router(h, g, wr, br, *, n_experts, tm):
    n, d = h.shape
    return pl.pallas_call(
        functools.partial(_router_kernel, n_experts=n_experts),
        grid=(n // tm,),
        in_specs=[
            pl.BlockSpec((tm, d), lambda i: (i, 0)),
            pl.BlockSpec((1, d), lambda i: (0, 0)),
            pl.BlockSpec((d, LANES), lambda i: (0, 0)),
            pl.BlockSpec((1, LANES), lambda i: (0, 0)),
        ],
        out_specs=[
            pl.BlockSpec((tm, LANES), lambda i: (i, 0)),
            pl.BlockSpec((tm, LANES), lambda i: (i, 0)),
            pl.BlockSpec((8, LANES), lambda i: (0, 0)),
        ],
        out_shape=[
            jax.ShapeDtypeStruct((n, LANES), I32),
            jax.ShapeDtypeStruct((n, LANES), F32),
            jax.ShapeDtypeStruct((8, LANES), F32),
        ],
        compiler_params=_cparams(("arbitrary",)),
        name="moe_router",
    )(h, g, wr, br)


def _slots_kernel(idx_ref, cnt_ref, dest_ref, be_ref, meta_ref, carry, *, n_experts, tm_e, nb_pad):
    t = idx_ref.shape[0]
    lane1 = lax.broadcasted_iota(I32, (1, LANES), 1)

    @pl.when(pl.program_id(0) == 0)
    def _():
        cnt = cnt_ref[0:1, :]
        padded = jnp.floor((cnt + (tm_e - 1)) / tm_e) * tm_e
        er = lax.broadcasted_iota(I32, (LANES, LANES), 0)
        ec = lax.broadcasted_iota(I32, (LANES, LANES), 1)
        start = _dot01_right(jnp.broadcast_to(padded, (8, LANES)), (er < ec).astype(BF16))[0:1, :]
        carry[...] = start
        pend = start + padded
        jrow = (lax.broadcasted_iota(I32, (nb_pad, LANES), 0) * tm_e).astype(F32)
        lane = lax.broadcasted_iota(I32, (nb_pad, LANES), 1)
        hit = jnp.where((pend <= jrow) & (lane < n_experts), 1.0, 0.0)
        be = jnp.minimum(jnp.sum(hit, axis=-1, keepdims=True), n_experts - 1.0)
        be_ref[...] = jnp.broadcast_to(be, (nb_pad, LANES)).astype(I32)
        total = jnp.sum(jnp.where(lane1 < n_experts, padded, 0.0), axis=-1, keepdims=True)
        meta = jnp.concatenate([start, cnt, pend, jnp.broadcast_to(total / tm_e, (1, LANES)),
                                jnp.zeros((4, LANES), F32)], axis=0)
        meta_ref[...] = meta.astype(I32)

    idx = idx_ref[...]
    lane = lax.broadcasted_iota(I32, (t, LANES), 1)
    ohs = [lane == idx[:, k:k + 1] for k in range(TOP_K)]
    oh = jnp.zeros((t, LANES), F32)
    for o in ohs:
        oh = oh + o.astype(F32)
    r = lax.broadcasted_iota(I32, (t, t), 0)
    c = lax.broadcasted_iota(I32, (t, t), 1)
    earlier = jnp.dot((c < r).astype(BF16), oh.astype(BF16), preferred_element_type=F32)
    base = carry[...] + earlier
    dest = jnp.zeros((t, LANES), F32)
    for k in range(TOP_K):
        dk = jnp.sum(jnp.where(ohs[k], base, 0.0), axis=-1, keepdims=True)
        dest = jnp.where(lane == k, dk, dest)
    dest_ref[...] = dest.astype(I32)
    carry[...] += jnp.sum(oh, axis=0, keepdims=True)


def _slots(idx, cnt, *, n_experts, tm_e, nb_pad, t):
    n = idx.shape[0]
    return pl.pallas_call(
        functools.partial(_slots_kernel, n_experts=n_experts, tm_e=tm_e, nb_pad=nb_pad),
        grid=(n // t,),
        in_specs=[pl.BlockSpec((t, LANES), lambda i: (i, 0)), pl.BlockSpec((8, LANES), lambda i: (0, 0))],
        out_specs=[
            pl.BlockSpec((t, LANES), lambda i: (i, 0)),
            pl.BlockSpec((nb_pad, LANES), lambda i: (0, 0)),
            pl.BlockSpec((8, LANES), lambda i: (0, 0)),
        ],
        out_shape=[
            jax.ShapeDtypeStruct((n, LANES), I32),
            jax.ShapeDtypeStruct((nb_pad, LANES), I32),
            jax.ShapeDtypeStruct((8, LANES), I32),
        ],
        scratch_shapes=[pltpu.VMEM((1, LANES), F32)],
        compiler_params=_cparams(("arbitrary",)),
        name="moe_slots",
    )(idx, cnt)


def _drain(copy, n):
    assert n % DRAIN_UNROLL == 0

    def body(_, c):
        for _ in range(DRAIN_UNROLL):
            copy.wait()
        return c

    lax.fori_loop(0, n // DRAIN_UNROLL, body, 0)


def _dispatch_kernel(dest_sm, fill_lo_sm, fill_hi_sm, h_ref, g_ref, buf_hbm, urows, zrow, sem, zsem, *, t, n_experts):
    i = pl.program_id(0)
    last = pl.num_programs(0) - 1
    cur = i % 2

    def row_copy(buf, j, slot):
        return pltpu.make_async_copy(urows.at[buf, pl.ds(j, 1)], buf_hbm.at[pl.ds(slot, 1)], sem.at[buf])

    def zero_copy(slot):
        return pltpu.make_async_copy(zrow.at[pl.ds(0, 1)], buf_hbm.at[pl.ds(slot, 1)], zsem)

    @pl.when(i == 0)
    def _():
        zrow[...] = jnp.zeros_like(zrow)
        for e in range(n_experts):
            lo, hi = fill_lo_sm[e], fill_hi_sm[e]

            def zstart(s, c):
                zero_copy(s).start()
                return c

            def zwait(s, c):
                zero_copy(s).wait()
                return c

            lax.fori_loop(lo, hi, zstart, 0)
            lax.fori_loop(lo, hi, zwait, 0)

        def tail_copy(b):
            s = pl.multiple_of(fill_hi_sm[n_experts - 1] + b * zrow.shape[0], zrow.shape[0])
            return pltpu.make_async_copy(zrow, buf_hbm.at[pl.ds(s, zrow.shape[0])], zsem)

        n_tail = (buf_hbm.shape[0] - fill_hi_sm[n_experts - 1]) // zrow.shape[0]

        def tstart(b, c):
            tail_copy(b).start()
            return c

        def twait(b, c):
            tail_copy(b).wait()
            return c

        lax.fori_loop(0, n_tail, tstart, 0)
        lax.fori_loop(0, n_tail, twait, 0)

    base = i * t
    urows[cur] = _pack_bf16_pairs(_rms(h_ref[...], g_ref[...]))

    def issue(jb, c):
        j0 = pl.multiple_of(jb * ISSUE_GROUP, ISSUE_GROUP)
        for r in range(ISSUE_GROUP):
            for k in range(TOP_K):
                row_copy(cur, j0 + r, dest_sm[(base + j0 + r) * TOP_K + k]).start(priority=k % 2)
        return c

    lax.fori_loop(0, t // ISSUE_GROUP, issue, 0)

    @pl.when(i > 0)
    def _():
        _drain(row_copy(1 - cur, 0, 0), t * TOP_K)

    @pl.when(i == last)
    def _():
        _drain(row_copy(cur, 0, 0), t * TOP_K)


def _dispatch(dest_flat, fill_lo, fill_hi, h, g, *, n_slots, n_experts, t):
    n, d = h.shape
    return pl.pallas_call(
        functools.partial(_dispatch_kernel, t=t, n_experts=n_experts),
        grid_spec=pltpu.PrefetchScalarGridSpec(
            num_scalar_prefetch=3,
            grid=(n // t,),
            in_specs=[pl.BlockSpec((t, d), lambda i, *_: (i, 0)), pl.BlockSpec((1, d), lambda i, *_: (0, 0))],
            out_specs=pl.BlockSpec(memory_space=pl.ANY),
            scratch_shapes=[pltpu.VMEM((2, t, d // 2), U32), pltpu.VMEM((8, d // 2), U32),
                            pltpu.SemaphoreType.DMA((2,)), pltpu.SemaphoreType.DMA(())],
        ),
        out_shape=jax.ShapeDtypeStruct((n_slots, d // 2), U32),
        compiler_params=_cparams(("arbitrary",)),
        name="moe_dispatch",
    )(dest_flat, fill_lo, fill_hi, h, g)


def _pack_bf16_pairs(x):
    w = x.shape[1] // 2
    bits = lambda a: lax.bitcast_convert_type(a.astype(BF16).astype(F32), U32)
    return (bits(x[:, :w]) >> 16) | (bits(x[:, w:]) & jnp.uint32(0xFFFF0000))


def _unpack_bf16_pairs(p):
    lo = lax.bitcast_convert_type(p << 16, F32)
    hi = lax.bitcast_convert_type(p & jnp.uint32(0xFFFF0000), F32)
    return lo, hi


def _expert_kernel(blk_sm, kind_sm, exp_sm, wch_sm, x_ref, w1g_ref, w1u_ref, b1_ref, w2_ref, b2_ref, o_ref,
                   xb, acc, w1g_res, w1u_res, w2_res, *, nc, tc):
    kind = kind_sm[pl.program_id(0)]
    half = xb.shape[1] // 2
    de = nc * tc

    def begin():
        lo, hi = _unpack_bf16_pairs(x_ref[...])
        xb[:, :half] = lo.astype(BF16)
        xb[:, half:] = hi.astype(BF16)
        acc[...] = jnp.broadcast_to(b2_ref[...], acc.shape)

    def chunk(c):
        x = xb[...]
        c0 = pl.multiple_of(c * tc, tc)
        g = jnp.dot(x, w1g_res[c], preferred_element_type=F32) + b1_ref[:, pl.ds(c0, tc)]
        u = jnp.dot(x, w1u_res[c], preferred_element_type=F32) + b1_ref[:, pl.ds(de + c0, tc)]
        g = jnp.minimum(g, SWIGLU_LIMIT)
        u = jnp.clip(u, -SWIGLU_LIMIT, SWIGLU_LIMIT)
        act = g * jax.nn.sigmoid(SWIGLU_ALPHA * g) * (u + 1.0)
        acc[...] += jnp.dot(act.astype(BF16), w2_res[c], preferred_element_type=F32)

    def finish():
        o_ref[...] = _pack_bf16_pairs(acc[...])

    @pl.when(kind < nc)
    def _():
        w1g_res[kind] = w1g_ref[...].astype(BF16)
        w1u_res[kind] = w1u_ref[...].astype(BF16)
        w2_res[kind] = w2_ref[...].astype(BF16)
        pl.when(kind == 0)(begin)
        chunk(kind)
        pl.when(kind == nc - 1)(finish)

    @pl.when(kind == nc)
    def _():
        begin()

        def body(c, carry):
            chunk(c)
            return carry

        lax.fori_loop(0, nc, body, 0)
        finish()

    @pl.when(kind == nc + 1)
    def _():
        o_ref[...] = jnp.zeros_like(o_ref)


def _expert_items(blk_expert, n_used, *, n_experts, nc):
    nb = blk_expert.shape[0]
    n_items = n_experts * nc + nb - n_experts
    j = jnp.arange(nb, dtype=I32)
    used = j < n_used
    prev = jnp.concatenate([blk_expert[:1] - 1, blk_expert[:-1]])
    first = used & (blk_expert != prev)
    per_block = jnp.where(first, nc, 1).astype(I32)
    ends = jnp.cumsum(per_block)
    starts = ends - per_block
    i = jnp.arange(n_items, dtype=I32)
    blk = jnp.minimum(jnp.searchsorted(ends, i, side="right").astype(I32), nb - 1)
    within = i - starts[blk]
    kind = jnp.where(i >= ends[nb - 1], nc + 2,
                     jnp.where(~used[blk], nc + 1, jnp.where(first[blk], within, nc))).astype(I32)
    exp = blk_expert[jnp.minimum(blk, n_used - 1)]
    wch = jnp.where(kind < nc, kind, nc - 1).astype(I32)
    return blk, kind, exp, wch


def _experts(blk_expert, n_used, buf, w1, b1, w2, b2, *, tm, tc):
    n_slots, half = buf.shape
    d = 2 * half
    ne, _, two_de = w1.shape
    de = two_de // 2
    nc = de // tc
    items = _expert_items(blk_expert, n_used, n_experts=ne, nc=nc)
    n_items = items[0].shape[0]
    return pl.pallas_call(
        functools.partial(_expert_kernel, nc=nc, tc=tc),
        grid_spec=pltpu.PrefetchScalarGridSpec(
            num_scalar_prefetch=4,
            grid=(n_items,),
            in_specs=[
                pl.BlockSpec((tm, half), lambda i, blk, kind, exp, wch: (blk[i], 0)),
                pl.BlockSpec((None, d, tc), lambda i, blk, kind, exp, wch: (exp[i], 0, wch[i])),
                pl.BlockSpec((None, d, tc), lambda i, blk, kind, exp, wch: (exp[i], 0, nc + wch[i])),
                pl.BlockSpec((None, 1, two_de), lambda i, blk, kind, exp, wch: (exp[i], 0, 0)),
                pl.BlockSpec((None, tc, d), lambda i, blk, kind, exp, wch: (exp[i], wch[i], 0)),
                pl.BlockSpec((None, 1, d), lambda i, blk, kind, exp, wch: (exp[i], 0, 0)),
            ],
            out_specs=pl.BlockSpec((tm, half), lambda i, blk, kind, exp, wch: (blk[i], 0)),
            scratch_shapes=[
                pltpu.VMEM((tm, d), BF16), pltpu.VMEM((tm, d), F32),
                pltpu.VMEM((nc, d, tc), BF16), pltpu.VMEM((nc, d, tc), BF16), pltpu.VMEM((nc, tc, d), BF16),
            ],
        ),
        out_shape=jax.ShapeDtypeStruct((n_slots, half), U32),
        compiler_params=_cparams(("arbitrary",)),
        name="moe_experts",
    )(*items, buf, w1, w1, b1, w2, b2)


def _combine_kernel(dest_sm, gate_ref, h_ref, g_ref, y_hbm, o_ref, rows, sem, *, t):
    i = pl.program_id(0)
    last = pl.num_programs(0) - 1
    cur = i % 2

    def row_copy(buf, slot, k, j):
        return pltpu.make_async_copy(y_hbm.at[pl.ds(slot, 1)], rows.at[buf, k, pl.ds(j, 1)], sem.at[buf])

    def gather(tile, buf):
        def issue(jb, c):
            j0 = pl.multiple_of(jb * ISSUE_GROUP, ISSUE_GROUP)
            for r in range(ISSUE_GROUP):
                for k in range(TOP_K):
                    row_copy(buf, dest_sm[(tile * t + j0 + r) * TOP_K + k], k, j0 + r).start(priority=k % 2)
            return c

        lax.fori_loop(0, t // ISSUE_GROUP, issue, 0)

    @pl.when(i == 0)
    def _():
        gather(0, 0)

    @pl.when(i < last)
    def _():
        gather(i + 1, 1 - cur)

    _drain(row_copy(cur, 0, 0, 0), t * TOP_K)
    gate = gate_ref[...]
    half = h_ref.shape[1] // 2
    y_lo = h_ref[:, :half]
    y_hi = h_ref[:, half:]
    for k in range(TOP_K):
        lo, hi = _unpack_bf16_pairs(rows[cur, k])
        y_lo = y_lo + gate[:, k:k + 1] * lo
        y_hi = y_hi + gate[:, k:k + 1] * hi
    o_ref[...] = _rms(jnp.concatenate([y_lo, y_hi], axis=1), g_ref[...])


def _combine(dest_flat, gate, h, g, ybuf, *, t):
    n, d = h.shape
    return pl.pallas_call(
        functools.partial(_combine_kernel, t=t),
        grid_spec=pltpu.PrefetchScalarGridSpec(
            num_scalar_prefetch=1,
            grid=(n // t,),
            in_specs=[
                pl.BlockSpec((t, LANES), lambda i, ds: (i, 0)),
                pl.BlockSpec((t, d), lambda i, ds: (i, 0)),
                pl.BlockSpec((1, d), lambda i, ds: (0, 0)),
                pl.BlockSpec(memory_space=pl.ANY),
            ],
            out_specs=pl.BlockSpec((t, d), lambda i, ds: (i, 0)),
            scratch_shapes=[pltpu.VMEM((2, TOP_K, t, d // 2), U32), pltpu.SemaphoreType.DMA((2,))],
        ),
        out_shape=jax.ShapeDtypeStruct((n, d), F32),
        compiler_params=_cparams(("arbitrary",)),
        name="moe_combine",
    )(dest_flat, gate, h, g, ybuf)


def _pad_lanes(v, lane0=0):
    out = jnp.zeros((1, LANES), F32)
    return out.at[0, lane0:lane0 + v.shape[0]].set(v.astype(F32))


def kernel(x, mem, ln_mix, w_in, conv_w, conv_b, dt_bias, a_log, d_skip, ssd_norm, fgate_bias, w_out, ln_xa, ln_mem,
           w_xq, w_xkv, w_xo, ln_ffn, w_router, b_router, w_moe1, b_moe1, w_moe2, b_moe2, ln_final):
    batch, seq, d = x.shape
    mem_len = mem.shape[1]
    depth = ln_mix.shape[0]
    n_ssd_heads = dt_bias.shape[1]
    n_fox_heads = fgate_bias.shape[1]
    d_ssd = n_ssd_heads * SSD_HEAD_DIM
    d_fox = n_fox_heads * FOX_HEAD_DIM
    cdim = d_ssd + 2 * SSD_GROUPS * SSD_STATE
    n_experts = w_router.shape[2]
    n = batch * seq
    fox_lane0 = n_ssd_heads
    assert n_ssd_heads + n_fox_heads <= LANES and cdim % d_ssd == 0 and d_ssd == d and d_fox == d

    tm = min(1024, n)
    o_z, o_xbc, o_dt = 0, d_ssd, d_ssd + cdim
    o_q = o_dt + n_ssd_heads
    o_k, o_v, o_f = o_q + d_fox, o_q + 2 * d_fox, o_q + 3 * d_fox
    q_col = (cdim + d_ssd) // FOX_HEAD_DIM
    k_col = q_col + n_fox_heads
    v_col = k_col + n_fox_heads

    head_of_ch = jnp.arange(d_ssd, dtype=I32) // SSD_HEAD_DIM
    e01 = (jnp.arange(LANES, dtype=I32)[:, None] == head_of_ch[None, :]).astype(BF16)

    n_assign = n * TOP_K
    nb = -(-n_assign // MOE_TM) + n_experts
    nb_pad = -(-nb // 8) * 8
    n_slots = nb * MOE_TM

    h = x.reshape(n, d)
    mem2 = mem.reshape(batch * mem_len, d)
    for i in range(depth):
        w = w_in[i]
        w_main = jnp.concatenate([w[:, o_xbc:o_dt], w[:, o_z:o_xbc], w[:, o_q:o_f]], axis=1).astype(BF16)
        w_small = jnp.concatenate(
            [w[:, o_dt:o_q], w[:, o_f:], jnp.zeros((d, LANES - n_ssd_heads - n_fox_heads), F32)], axis=1).astype(BF16)
        pmain, psmall = _inproj(h, ln_mix[i][None, :], w_main, w_small, tm, 1024)

        y_ssd, fcol, frow = _ssd(
            pmain, psmall, conv_w[i], conv_b[i][None, :], _pad_lanes(dt_bias[i]), _pad_lanes(a_log[i]),
            _pad_lanes(fgate_bias[i], fox_lane0), jnp.repeat(d_skip[i].astype(F32), SSD_HEAD_DIM)[None, :],
            ssd_norm[i][None, :], e01, batch=batch, seq=seq, d_inner=d_ssd, n_heads=n_ssd_heads, fox_lane0=fox_lane0)
        y_fox = _fox(pmain, fcol, frow, batch=batch, seq=seq, n_heads=n_fox_heads, q_col=q_col, k_col=k_col,
                     v_col=v_col, fox_lane0=fox_lane0, tq=min(256, seq))
        h = _mm_res([y_ssd, y_fox], w_out[i].astype(BF16), h, tm, 512, "out_proj")

        q = _norm_matmul(h, ln_xa[i][None, :], w_xq[i].astype(BF16), tm, 1024, "xa_q")
        kv = _norm_matmul(mem2, ln_mem[i][None, :], w_xkv[i].astype(BF16), min(1024, batch * mem_len), 1024, "xa_kv")
        o = _xattn(q, kv, batch=batch, seq=seq, mem_len=mem_len, d_model=d, tq=min(512, seq))
        h = _mm_res([o], w_xo[i].astype(BF16), h, tm, 512, "xa_o")

        wr = jnp.concatenate([w_router[i], jnp.zeros((d, LANES - n_experts), F32)], axis=1)
        idx, gate, cnt = _router(h, ln_ffn[i][None, :], wr, _pad_lanes(b_router[i]), n_experts=n_experts,
                                 tm=min(512, n))
        dest, be, meta = _slots(idx, cnt, n_experts=n_experts, tm_e=MOE_TM, nb_pad=nb_pad, t=min(512, n))
        dest_flat = dest[:, :TOP_K].reshape(-1)
        fill_lo = meta[0, :n_experts] + meta[1, :n_experts]
        fill_hi = meta[2, :n_experts]
        buf = _dispatch(dest_flat, fill_lo, fill_hi, h, ln_ffn[i][None, :], n_slots=n_slots, n_experts=n_experts,
                        t=min(256, n))
        ybuf = _experts(be[:nb, 0], meta[3, :1], buf, w_moe1[i], b_moe1[i][:, None, :], w_moe2[i],
                        b_moe2[i][:, None, :], tm=MOE_TM, tc=MOE_TC)
        last = i == depth - 1
        assert last, "the fused final norm assumes a single layer"
        h = _combine(dest_flat, gate, h, ln_final[None, :], ybuf, t=min(256, n))
    return h.reshape(batch, seq, d)
```

```python
import functools

import jax
import jax.numpy as jnp
from jax import lax
from jax.experimental import pallas as pl
from jax.experimental.pallas import tpu as pltpu

F32 = jnp.float32
BF16 = jnp.bfloat16
I32 = jnp.int32
U32 = jnp.uint32

RMS_EPS = 1e-5
LANES = 128
SSD_HEAD_DIM = 64
SSD_GROUPS = 8
SSD_STATE = 128
SSD_CONV = 4
SSD_CHUNK = 128
CONV_HALO = 8
FOX_HEAD_DIM = 128
XA_HEADS = 4
TOP_K = 4
SWIGLU_LIMIT = 7.0
SWIGLU_ALPHA = 1.702
NEG_BIG = -1e30
VMEM_LIMIT = 56 * 1024 * 1024

MOE_TM = 512
MOE_TC = 256
MOE_ROW_TILE = 512
DRAIN_UNROLL = 32
ISSUE_GROUP = 8


def _cparams(sem):
    return pltpu.CompilerParams(dimension_semantics=sem, vmem_limit_bytes=VMEM_LIMIT)


def _rms(x, g):
    ms = jnp.mean(x * x, axis=-1, keepdims=True)
    return x * lax.rsqrt(ms + RMS_EPS) * g


def _softplus(x):
    return jnp.maximum(x, 0.0) + jnp.log1p(jnp.exp(-jnp.abs(x)))


def _split3(x):
    hi = x.astype(BF16)
    r1 = x - hi.astype(F32)
    mid = r1.astype(BF16)
    lo = (r1 - mid.astype(F32)).astype(BF16)
    return hi, mid, lo


def _dot01_right(x, m01):
    hi, mid, lo = _split3(x)
    d = lambda a: jnp.dot(a, m01, preferred_element_type=F32)
    return d(hi) + d(mid) + d(lo)


def _dot01_left(m01, x):
    hi, mid, lo = _split3(x)
    d = lambda a: jnp.dot(m01, a, preferred_element_type=F32)
    return d(hi) + d(mid) + d(lo)


def _inproj_kernel(x_ref, g_ref, w_ref, ws_ref, o_ref, os_ref, u_scr):
    @pl.when(pl.program_id(1) == 0)
    def _():
        u = _rms(x_ref[...], g_ref[...]).astype(BF16)
        u_scr[...] = u
        os_ref[...] = jnp.dot(u, ws_ref[...], preferred_element_type=F32)

    o_ref[...] = jnp.dot(u_scr[...], w_ref[...], preferred_element_type=F32).astype(o_ref.dtype)


def _inproj(x, g, w_main, w_small, tm, tn):
    m, k = x.shape
    n = w_main.shape[1]
    return pl.pallas_call(
        _inproj_kernel,
        grid=(m // tm, n // tn),
        in_specs=[
            pl.BlockSpec((tm, k), lambda i, j: (i, 0)),
            pl.BlockSpec((1, k), lambda i, j: (0, 0)),
            pl.BlockSpec((k, tn), lambda i, j: (0, j)),
            pl.BlockSpec((k, LANES), lambda i, j: (0, 0)),
        ],
        out_specs=[
            pl.BlockSpec((tm, tn), lambda i, j: (i, j)),
            pl.BlockSpec((tm, LANES), lambda i, j: (i, 0)),
        ],
        out_shape=[jax.ShapeDtypeStruct((m, n), BF16), jax.ShapeDtypeStruct((m, LANES), F32)],
        scratch_shapes=[pltpu.VMEM((tm, k), BF16)],
        compiler_params=_cparams(("parallel", "arbitrary")),
        name="inproj",
    )(x, g, w_main, w_small)


def _norm_matmul_kernel(x_ref, g_ref, w_ref, o_ref, u_scr):
    @pl.when(pl.program_id(1) == 0)
    def _():
        u_scr[...] = _rms(x_ref[...], g_ref[...]).astype(BF16)

    o_ref[...] = jnp.dot(u_scr[...], w_ref[...], preferred_element_type=F32).astype(o_ref.dtype)


def _norm_matmul(x, g, w, tm, tn, name):
    m, k = x.shape
    n = w.shape[1]
    return pl.pallas_call(
        _norm_matmul_kernel,
        grid=(m // tm, n // tn),
        in_specs=[
            pl.BlockSpec((tm, k), lambda i, j: (i, 0)),
            pl.BlockSpec((1, k), lambda i, j: (0, 0)),
            pl.BlockSpec((k, tn), lambda i, j: (0, j)),
        ],
        out_specs=pl.BlockSpec((tm, tn), lambda i, j: (i, j)),
        out_shape=jax.ShapeDtypeStruct((m, n), BF16),
        scratch_shapes=[pltpu.VMEM((tm, k), BF16)],
        compiler_params=_cparams(("parallel", "arbitrary")),
        name=name,
    )(x, g, w)


def _mm_res_kernel(*refs, n_lhs):
    res_ref, o_ref = refs[2 * n_lhs], refs[2 * n_lhs + 1]
    acc = res_ref[...]
    for a_ref, w_ref in zip(refs[:n_lhs], refs[n_lhs:2 * n_lhs]):
        acc = acc + jnp.dot(a_ref[...], w_ref[...], preferred_element_type=F32)
    o_ref[...] = acc


def _mm_res(lhs, w, res, tm, tn, name):
    m, n = res.shape
    n_lhs = len(lhs)
    kb = lhs[0].shape[1]
    assert all(a.shape[1] == kb for a in lhs) and w.shape[0] == n_lhs * kb
    ws = [w] * n_lhs
    in_specs = [pl.BlockSpec((tm, kb), lambda i, j: (i, 0)) for _ in lhs]
    in_specs += [pl.BlockSpec((kb, tn), lambda i, j, r=r: (r, j)) for r in range(n_lhs)]
    in_specs += [pl.BlockSpec((tm, tn), lambda i, j: (i, j))]
    return pl.pallas_call(
        functools.partial(_mm_res_kernel, n_lhs=n_lhs),
        grid=(m // tm, n // tn),
        in_specs=in_specs,
        out_specs=pl.BlockSpec((tm, tn), lambda i, j: (i, j)),
        out_shape=jax.ShapeDtypeStruct((m, n), F32),
        compiler_params=_cparams(("parallel", "arbitrary")),
        name=name,
    )(*lhs, *ws, res)


def _ssd_kernel(xbc_ref, z_ref, sm_ref, cw_ref, cb_ref, dtb_ref, alog_ref, fb_ref, dch_ref, gn_ref, e_ref,
                y_ref, fcol_ref, frow_ref,
                ext, cv, state, fcarry, yacc, dtch, csch, *, d_inner, n_heads, fox_lane0):
    L = SSD_CHUNK
    gw = d_inner // SSD_GROUPS
    hpg = gw // SSD_HEAD_DIM
    cdim = d_inner + 2 * SSD_GROUPS * SSD_STATE
    strip = 256

    @pl.when(pl.program_id(1) == 0)
    def _():
        ext[...] = jnp.zeros_like(ext)
        state[...] = jnp.zeros_like(state)
        fcarry[...] = jnp.zeros_like(fcarry)

    row = lax.broadcasted_iota(I32, (L, L), 0)
    col = lax.broadcasted_iota(I32, (L, L), 1)
    causal = col <= row
    tri = causal.astype(BF16)

    shifts = [(col == row - (SSD_CONV - 1 - k)).astype(BF16) for k in range(SSD_CONV - 1)]
    for s0 in range(0, cdim, strip):
        cur = xbc_ref[:, s0:s0 + strip]
        acc = cb_ref[:, s0:s0 + strip] + cw_ref[SSD_CONV - 1:SSD_CONV, s0:s0 + strip] * cur.astype(F32)
        head = jnp.zeros((CONV_HALO, strip), F32)
        for k in range(SSD_CONV - 1):
            wk = cw_ref[k:k + 1, s0:s0 + strip]
            acc = acc + wk * jnp.dot(shifts[k], cur, preferred_element_type=F32)
            r0 = CONV_HALO - (SSD_CONV - 1) + k
            head = head + wk * ext[r0:r0 + CONV_HALO, s0:s0 + strip]
        acc = jnp.concatenate([acc[0:CONV_HALO] + head, acc[CONV_HALO:]], axis=0)
        cv[:, s0:s0 + strip] = acc * jax.nn.sigmoid(acc)
    ext[0:CONV_HALO, :] = xbc_ref[L - CONV_HALO:L, :].astype(F32)

    sm = sm_ref[...]
    dt = _softplus(sm + dtb_ref[...])
    dta = dt * (-jnp.exp(alog_ref[...]))
    cs = _dot01_left(tri, dta)
    cst = cs.T
    logf = -_softplus(-(sm + fb_ref[...]))
    fc = _dot01_left(tri, logf) + fcarry[...]
    fcol_ref[...] = fc
    frow_ref[...] = fc.T
    fcarry[...] = fc[L - 1:L, :]

    e01 = e_ref[...]
    dtch[...] = _dot01_right(dt, e01)
    csch[...] = _dot01_right(cs, e01)

    lane_g = lax.broadcasted_iota(I32, (1, gw), 1)
    for g in range(SSD_GROUPS):
        c0 = g * gw
        bg = cv[:, d_inner + g * SSD_STATE:d_inner + (g + 1) * SSD_STATE]
        cg = cv[:, d_inner + SSD_GROUPS * SSD_STATE + g * SSD_STATE:d_inner + SSD_GROUPS * SSD_STATE + (g + 1) * SSD_STATE]
        cgb = cg.astype(BF16)
        cbm = lax.dot_general(cgb, bg.astype(BF16), (((1,), (1,)), ((), ())), preferred_element_type=F32)
        bgt = bg.T.astype(BF16)
        xg = cv[:, c0:c0 + gw]
        csg = csch[:, c0:c0 + gw]
        cs_last = csg[L - 1:L, :]
        xdt = xg * dtch[:, c0:c0 + gw]
        xw = (xdt * jnp.exp(cs_last - csg)).astype(BF16)
        st_new = jnp.dot(bgt, xw, preferred_element_type=F32)
        prev = state[g]
        yg = jnp.dot(cgb, prev.astype(BF16), preferred_element_type=F32) * jnp.exp(csg)
        state[g] = prev * jnp.exp(cs_last) + st_new
        yg = yg + xg * dch_ref[:, c0:c0 + gw]
        for r in range(hpg):
            h = g * hpg + r
            seg = cs[:, h:h + 1] - cst[h:h + 1, :]
            m = (cbm * jnp.exp(jnp.where(causal, seg, NEG_BIG))).astype(BF16)
            head = (lane_g >= r * SSD_HEAD_DIM) & (lane_g < (r + 1) * SSD_HEAD_DIM)
            xh = jnp.where(head, xdt, 0.0).astype(BF16)
            yg = yg + jnp.dot(m, xh, preferred_element_type=F32)
        yacc[:, c0:c0 + gw] = yg

    z = z_ref[...].astype(F32)
    gated = yacc[...] * (z * jax.nn.sigmoid(z))
    y_ref[...] = _rms(gated, gn_ref[...]).astype(y_ref.dtype)


def _ssd(pmain, psmall, conv_w, conv_b, dtb, alog, fb, dch, gnorm, e01, *, batch, seq, d_inner, n_heads, fox_lane0):
    nc = seq // SSD_CHUNK
    cdim = d_inner + 2 * SSD_GROUPS * SSD_STATE
    gw = d_inner // SSD_GROUPS
    m = batch * seq
    rowblk = lambda b, c: b * nc + c
    const = lambda shape: pl.BlockSpec(shape, lambda b, c: (0, 0))
    kern = functools.partial(_ssd_kernel, d_inner=d_inner, n_heads=n_heads, fox_lane0=fox_lane0)
    return pl.pallas_call(
        kern,
        grid=(batch, nc),
        in_specs=[
            pl.BlockSpec((SSD_CHUNK, cdim), lambda b, c: (rowblk(b, c), 0)),
            pl.BlockSpec((SSD_CHUNK, d_inner), lambda b, c: (rowblk(b, c), cdim // d_inner)),
            pl.BlockSpec((SSD_CHUNK, LANES), lambda b, c: (rowblk(b, c), 0)),
            const((SSD_CONV, cdim)), const((1, cdim)), const((1, LANES)), const((1, LANES)), const((1, LANES)),
            const((1, d_inner)), const((1, d_inner)), const((LANES, d_inner)),
        ],
        out_specs=[
            pl.BlockSpec((SSD_CHUNK, d_inner), lambda b, c: (rowblk(b, c), 0)),
            pl.BlockSpec((SSD_CHUNK, LANES), lambda b, c: (rowblk(b, c), 0)),
            pl.BlockSpec((None, LANES, SSD_CHUNK), lambda b, c: (b, 0, c)),
        ],
        out_shape=[
            jax.ShapeDtypeStruct((m, d_inner), BF16),
            jax.ShapeDtypeStruct((m, LANES), F32),
            jax.ShapeDtypeStruct((batch, LANES, seq), F32),
        ],
        scratch_shapes=[
            pltpu.VMEM((2 * CONV_HALO, cdim), F32),
            pltpu.VMEM((SSD_CHUNK, cdim), F32),
            pltpu.VMEM((SSD_GROUPS, SSD_STATE, gw), F32),
            pltpu.VMEM((1, LANES), F32),
            pltpu.VMEM((SSD_CHUNK, d_inner), F32),
            pltpu.VMEM((SSD_CHUNK, d_inner), F32),
            pltpu.VMEM((SSD_CHUNK, d_inner), F32),
        ],
        compiler_params=_cparams(("arbitrary", "arbitrary")),
        name="ssd_scan",
    )(pmain, pmain, psmall, conv_w, conv_b, dtb, alog, fb, dch, gnorm, e01)


def _fox_kernel(q_ref, k_ref, v_ref, fcol_ref, frow_ref, o_ref, s_scr, fq_scr, m_scr, l_scr, acc_scr, *, tq, fox_lane0):
    h = pl.program_id(1)
    nq = q_ref.shape[0] // tq
    ng = tq // LANES
    scale = FOX_HEAD_DIM ** -0.5
    lane = lax.broadcasted_iota(I32, (tq, LANES), 1)
    causal = lax.broadcasted_iota(I32, (tq, tq), 1) <= lax.broadcasted_iota(I32, (tq, tq), 0)
    groups = lambda a: [a[:, g * LANES:(g + 1) * LANES] for g in range(ng)]

    for qi in range(nq):
        qs = qi * tq
        q = (q_ref[qs:qs + tq, :].astype(F32) * scale).astype(BF16)
        fq = jnp.sum(jnp.where(lane == fox_lane0 + h, fcol_ref[qs:qs + tq, :], 0.0), axis=-1, keepdims=True)
        fq_scr[...] = jnp.broadcast_to(fq, (tq, LANES))
        m_scr[...] = jnp.full((tq, LANES), NEG_BIG, F32)
        for kb in range(qi + 1):
            ks = kb * tq
            fk = frow_ref[pl.ds(fox_lane0 + h, 1), ks:ks + tq]
            s = lax.dot_general(q, k_ref[ks:ks + tq, :], (((1,), (1,)), ((), ())), preferred_element_type=F32)
            fqr = fq_scr[...]
            s = jnp.concatenate([sg + fqr for sg in groups(s)], axis=1) - fk
            if kb == qi:
                s = jnp.where(causal, s, NEG_BIG)
            s_scr[kb] = s
            m_scr[...] = functools.reduce(jnp.maximum, groups(s), m_scr[...])
        m_row = jnp.max(m_scr[...], axis=-1, keepdims=True)
        m_scr[...] = jnp.broadcast_to(m_row, (tq, LANES))
        l_scr[...] = jnp.zeros_like(l_scr)
        acc_scr[...] = jnp.zeros_like(acc_scr)
        for kb in range(qi + 1):
            ks = kb * tq
            mr = m_scr[...]
            ps = [jnp.exp(sg - mr) for sg in groups(s_scr[kb])]
            l_scr[...] += functools.reduce(jnp.add, ps)
            p = jnp.concatenate(ps, axis=1).astype(BF16)
            acc_scr[...] += jnp.dot(p, v_ref[ks:ks + tq, :], preferred_element_type=F32)
        l = jnp.sum(l_scr[...], axis=-1, keepdims=True)
        o_ref[qs:qs + tq, :] = (acc_scr[...] * (1.0 / l)).astype(o_ref.dtype)


def _fox(pmain, fcol, frow, *, batch, seq, n_heads, q_col, k_col, v_col, fox_lane0, tq):
    kern = functools.partial(_fox_kernel, tq=tq, fox_lane0=fox_lane0)
    head_block = lambda col0: pl.BlockSpec((seq, FOX_HEAD_DIM), lambda b, h: (b, col0 + h))
    return pl.pallas_call(
        kern,
        grid=(batch, n_heads),
        in_specs=[
            head_block(q_col), head_block(k_col), head_block(v_col),
            pl.BlockSpec((seq, LANES), lambda b, h: (b, 0)),
            pl.BlockSpec((None, LANES, seq), lambda b, h: (b, 0, 0)),
        ],
        out_specs=head_block(0),
        out_shape=jax.ShapeDtypeStruct((batch * seq, n_heads * FOX_HEAD_DIM), BF16),
        scratch_shapes=[
            pltpu.VMEM((seq // tq, tq, tq), F32),
            pltpu.VMEM((tq, LANES), F32),
            pltpu.VMEM((tq, LANES), F32),
            pltpu.VMEM((tq, LANES), F32),
            pltpu.VMEM((tq, FOX_HEAD_DIM), F32),
        ],
        compiler_params=_cparams(("parallel", "arbitrary")),
        name="fox_attn",
    )(pmain, pmain, pmain, fcol, frow)


def _xattn_kernel(q_ref, kv_ref, o_ref, *, d_model):
    hd = d_model // XA_HEADS
    scale = hd ** -0.5
    for h in range(XA_HEADS):
        q = q_ref[:, h * hd:(h + 1) * hd]
        k = kv_ref[:, h * hd:(h + 1) * hd]
        v = kv_ref[:, d_model + h * hd:d_model + (h + 1) * hd]
        s = lax.dot_general(q, k, (((1,), (1,)), ((), ())), preferred_element_type=F32) * scale
        p = jnp.exp(s - jnp.max(s, axis=-1, keepdims=True))
        p = p * (1.0 / jnp.sum(p, axis=-1, keepdims=True))
        o_ref[:, h * hd:(h + 1) * hd] = jnp.dot(p.astype(BF16), v, preferred_element_type=F32).astype(o_ref.dtype)


def _xattn(q, kv, *, batch, seq, mem_len, d_model, tq):
    nq = seq // tq
    return pl.pallas_call(
        functools.partial(_xattn_kernel, d_model=d_model),
        grid=(batch, nq),
        in_specs=[
            pl.BlockSpec((tq, d_model), lambda b, i: (b * nq + i, 0)),
            pl.BlockSpec((mem_len, 2 * d_model), lambda b, i: (b, 0)),
        ],
        out_specs=pl.BlockSpec((tq, d_model), lambda b, i: (b * nq + i, 0)),
        out_shape=jax.ShapeDtypeStruct((batch * seq, d_model), BF16),
        compiler_params=_cparams(("parallel", "arbitrary")),
        name="mem_xattn",
    )(q, kv)


def _router_kernel(h_ref, g_ref, wr_ref, br_ref, idx_ref, gate_ref, cnt_ref, *, n_experts):
    tm = h_ref.shape[0]

    @pl.when(pl.program_id(0) == 0)
    def _():
        cnt_ref[...] = jnp.zeros_like(cnt_ref)

    u = _rms(h_ref[...], g_ref[...])
    u_hi, u_mid, _ = _split3(u)
    w_hi, w_mid, _ = _split3(wr_ref[...])
    d = lambda a, b: jnp.dot(a, b, preferred_element_type=F32)
    logits = d(u_hi, w_hi) + (d(u_hi, w_mid) + d(u_mid, w_hi)) + br_ref[...]
    lane = lax.broadcasted_iota(I32, (tm, LANES), 1)
    work = jnp.where(lane < n_experts, logits, -jnp.inf)
    vals, idxs = [], []
    for _ in range(TOP_K):
        mx = jnp.max(work, axis=-1, keepdims=True)
        am = jnp.min(jnp.where(work == mx, lane, LANES), axis=-1, keepdims=True)
        vals.append(mx)
        idxs.append(am)
        work = jnp.where(lane == am, -jnp.inf, work)
    ex = [jnp.exp(v - vals[0]) for v in vals]
    den = ex[0] + ex[1] + ex[2] + ex[3]
    idx_out = jnp.zeros((tm, LANES), I32)
    gate_out = jnp.zeros((tm, LANES), F32)
    hits = jnp.zeros((tm, LANES), F32)
    for k in range(TOP_K):
        idx_out = jnp.where(lane == k, idxs[k], idx_out)
        gate_out = jnp.where(lane == k, ex[k] / den, gate_out)
        hits = hits + (lane == idxs[k]).astype(F32)
    idx_ref[...] = idx_out
    gate_ref[...] = gate_out
    cnt_ref[...] += jnp.broadcast_to(jnp.sum(hits, axis=0, keepdims=True), cnt_ref.shape)


def _router(h, g, wr, br, *, n_experts, tm):
    n, d = h.shape
    return pl.pallas_call(
        functools.partial(_router_kernel, n_experts=n_experts),
        grid=(n // tm,),
        in_specs=[
            pl.BlockSpec((tm, d), lambda i: (i, 0)),
            pl.BlockSpec((1, d), lambda i: (0, 0)),
            pl.BlockSpec((d, LANES), lambda i: (0, 0)),
            pl.BlockSpec((1, LANES), lambda i: (0, 0)),
        ],
        out_specs=[
            pl.BlockSpec((tm, LANES), lambda i: (i, 0)),
            pl.BlockSpec((tm, LANES), lambda i: (i, 0)),
            pl.BlockSpec((8, LANES), lambda i: (0, 0)),
        ],
        out_shape=[
            jax.ShapeDtypeStruct((n, LANES), I32),
            jax.ShapeDtypeStruct((n, LANES), F32),
            jax.ShapeDtypeStruct((8, LANES), F32),
        ],
        compiler_params=_cparams(("arbitrary",)),
        name="moe_router",
    )(h, g, wr, br)


def _slots_kernel(idx_ref, cnt_ref, dest_ref, be_ref, meta_ref, carry, *, n_experts, tm_e, nb_pad):
    t = idx_ref.shape[0]
    lane1 = lax.broadcasted_iota(I32, (1, LANES), 1)

    @pl.when(pl.program_id(0) == 0)
    def _():
        cnt = cnt_ref[0:1, :]
        padded = jnp.floor((cnt + (tm_e - 1)) / tm_e) * tm_e
        er = lax.broadcasted_iota(I32, (LANES, LANES), 0)
        ec = lax.broadcasted_iota(I32, (LANES, LANES), 1)
        start = _dot01_right(jnp.broadcast_to(padded, (8, LANES)), (er < ec).astype(BF16))[0:1, :]
        carry[...] = start
        pend = start + padded
        jrow = (lax.broadcasted_iota(I32, (nb_pad, LANES), 0) * tm_e).astype(F32)
        lane = lax.broadcasted_iota(I32, (nb_pad, LANES), 1)
        hit = jnp.where((pend <= jrow) & (lane < n_experts), 1.0, 0.0)
        be = jnp.minimum(jnp.sum(hit, axis=-1, keepdims=True), n_experts - 1.0)
        be_ref[...] = jnp.broadcast_to(be, (nb_pad, LANES)).astype(I32)
        total = jnp.sum(jnp.where(lane1 < n_experts, padded, 0.0), axis=-1, keepdims=True)
        meta = jnp.concatenate([start, cnt, pend, jnp.broadcast_to(total / tm_e, (1, LANES)),
                                jnp.zeros((4, LANES), F32)], axis=0)
        meta_ref[...] = meta.astype(I32)

    idx = idx_ref[...]
    lane = lax.broadcasted_iota(I32, (t, LANES), 1)
    ohs = [lane == idx[:, k:k + 1] for k in range(TOP_K)]
    oh = jnp.zeros((t, LANES), F32)
    for o in ohs:
        oh = oh + o.astype(F32)
    r = lax.broadcasted_iota(I32, (t, t), 0)
    c = lax.broadcasted_iota(I32, (t, t), 1)
    earlier = jnp.dot((c < r).astype(BF16), oh.astype(BF16), preferred_element_type=F32)
    base = carry[...] + earlier
    dest = jnp.zeros((t, LANES), F32)
    for k in range(TOP_K):
        dk = jnp.sum(jnp.where(ohs[k], base, 0.0), axis=-1, keepdims=True)
        dest = jnp.where(lane == k, dk, dest)
    dest_ref[...] = dest.astype(I32)
    carry[...] += jnp.sum(oh, axis=0, keepdims=True)


def _slots(idx, cnt, *, n_experts, tm_e, nb_pad, t):
    n = idx.shape[0]
    return pl.pallas_call(
        functools.partial(_slots_kernel, n_experts=n_experts, tm_e=tm_e, nb_pad=nb_pad),
        grid=(n // t,),
        in_specs=[pl.BlockSpec((t, LANES), lambda i: (i, 0)), pl.BlockSpec((8, LANES), lambda i: (0, 0))],
        out_specs=[
            pl.BlockSpec((t, LANES), lambda i: (i, 0)),
            pl.BlockSpec((nb_pad, LANES), lambda i: (0, 0)),
            pl.BlockSpec((8, LANES), lambda i: (0, 0)),
        ],
        out_shape=[
            jax.ShapeDtypeStruct((n, LANES), I32),
            jax.ShapeDtypeStruct((nb_pad, LANES), I32),
            jax.ShapeDtypeStruct((8, LANES), I32),
        ],
        scratch_shapes=[pltpu.VMEM((1, LANES), F32)],
        compiler_params=_cparams(("arbitrary",)),
        name="moe_slots",
    )(idx, cnt)


def _drain(copy, n):
    assert n % DRAIN_UNROLL == 0

    def body(_, c):
        for _ in range(DRAIN_UNROLL):
            copy.wait()
        return c

    lax.fori_loop(0, n // DRAIN_UNROLL, body, 0)


def _dispatch_kernel(dest_sm, fill_lo_sm, fill_hi_sm, h_ref, g_ref, buf_hbm, urows, zrow, sem, zsem, *, t, n_experts):
    i = pl.program_id(0)
    last = pl.num_programs(0) - 1
    cur = i % 2

    def row_copy(buf, j, slot):
        return pltpu.make_async_copy(urows.at[buf, pl.ds(j, 1)], buf_hbm.at[pl.ds(slot, 1)], sem.at[buf])

    def zero_copy(slot):
        return pltpu.make_async_copy(zrow.at[pl.ds(0, 1)], buf_hbm.at[pl.ds(slot, 1)], zsem)

    @pl.when(i == 0)
    def _():
        zrow[...] = jnp.zeros_like(zrow)
        for e in range(n_experts):
            lo, hi = fill_lo_sm[e], fill_hi_sm[e]

            def zstart(s, c):
                zero_copy(s).start()
                return c

            def zwait(s, c):
                zero_copy(s).wait()
                return c

            lax.fori_loop(lo, hi, zstart, 0)
            lax.fori_loop(lo, hi, zwait, 0)

        def tail_copy(b):
            s = pl.multiple_of(fill_hi_sm[n_experts - 1] + b * zrow.shape[0], zrow.shape[0])
            return pltpu.make_async_copy(zrow, buf_hbm.at[pl.ds(s, zrow.shape[0])], zsem)

        n_tail = (buf_hbm.shape[0] - fill_hi_sm[n_experts - 1]) // zrow.shape[0]

        def tstart(b, c):
            tail_copy(b).start()
            return c

        def twait(b, c):
            tail_copy(b).wait()
            return c

        lax.fori_loop(0, n_tail, tstart, 0)
        lax.fori_loop(0, n_tail, twait, 0)

    base = i * t
    urows[cur] = _pack_bf16_pairs(_rms(h_ref[...], g_ref[...]))

    def issue(jb, c):
        j0 = pl.multiple_of(jb * ISSUE_GROUP, ISSUE_GROUP)
        for r in range(ISSUE_GROUP):
            for k in range(TOP_K):
                row_copy(cur, j0 + r, dest_sm[(base + j0 + r) * TOP_K + k]).start(priority=k % 2)
        return c

    lax.fori_loop(0, t // ISSUE_GROUP, issue, 0)

    @pl.when(i > 0)
    def _():
        _drain(row_copy(1 - cur, 0, 0), t * TOP_K)

    @pl.when(i == last)
    def _():
        _drain(row_copy(cur, 0, 0), t * TOP_K)


def _dispatch(dest_flat, fill_lo, fill_hi, h, g, *, n_slots, n_experts, t):
    n, d = h.shape
    return pl.pallas_call(
        functools.partial(_dispatch_kernel, t=t, n_experts=n_experts),
        grid_spec=pltpu.PrefetchScalarGridSpec(
            num_scalar_prefetch=3,
            grid=(n // t,),
            in_specs=[pl.BlockSpec((t, d), lambda i, *_: (i, 0)), pl.BlockSpec((1, d), lambda i, *_: (0, 0))],
            out_specs=pl.BlockSpec(memory_space=pl.ANY),
            scratch_shapes=[pltpu.VMEM((2, t, d // 2), U32), pltpu.VMEM((8, d // 2), U32),
                            pltpu.SemaphoreType.DMA((2,)), pltpu.SemaphoreType.DMA(())],
        ),
        out_shape=jax.ShapeDtypeStruct((n_slots, d // 2), U32),
        compiler_params=_cparams(("arbitrary",)),
        name="moe_dispatch",
    )(dest_flat, fill_lo, fill_hi, h, g)


def _pack_bf16_pairs(x):
    w = x.shape[1] // 2
    bits = lambda a: lax.bitcast_convert_type(a.astype(BF16).astype(F32), U32)
    return (bits(x[:, :w]) >> 16) | (bits(x[:, w:]) & jnp.uint32(0xFFFF0000))


def _unpack_bf16_pairs(p):
    lo = lax.bitcast_convert_type(p << 16, F32)
    hi = lax.bitcast_convert_type(p & jnp.uint32(0xFFFF0000), F32)
    return lo, hi


def _expert_kernel(blk_sm, kind_sm, exp_sm, wch_sm, x_ref, w1g_ref, w1u_ref, b1_ref, w2_ref, b2_ref, o_ref,
                   xb, acc, w1g_res, w1u_res, w2_res, *, nc, tc):
    kind = kind_sm[pl.program_id(0)]
    half = xb.shape[1] // 2
    de = nc * tc

    def begin():
        lo, hi = _unpack_bf16_pairs(x_ref[...])
        xb[:, :half] = lo.astype(BF16)
        xb[:, half:] = hi.astype(BF16)
        acc[...] = jnp.broadcast_to(b2_ref[...], acc.shape)

    def chunk(c):
        x = xb[...]
        c0 = pl.multiple_of(c * tc, tc)
        g = jnp.dot(x, w1g_res[c], preferred_element_type=F32) + b1_ref[:, pl.ds(c0, tc)]
        u = jnp.dot(x, w1u_res[c], preferred_element_type=F32) + b1_ref[:, pl.ds(de + c0, tc)]
        g = jnp.minimum(g, SWIGLU_LIMIT)
        u = jnp.clip(u, -SWIGLU_LIMIT, SWIGLU_LIMIT)
        act = g * jax.nn.sigmoid(SWIGLU_ALPHA * g) * (u + 1.0)
        acc[...] += jnp.dot(act.astype(BF16), w2_res[c], preferred_element_type=F32)

    def finish():
        o_ref[...] = _pack_bf16_pairs(acc[...])

    @pl.when(kind < nc)
    def _():
        w1g_res[kind] = w1g_ref[...].astype(BF16)
        w1u_res[kind] = w1u_ref[...].astype(BF16)
        w2_res[kind] = w2_ref[...].astype(BF16)
        pl.when(kind == 0)(begin)
        chunk(kind)
        pl.when(kind == nc - 1)(finish)

    @pl.when(kind == nc)
    def _():
        begin()

        def body(c, carry):
            chunk(c)
            return carry

        lax.fori_loop(0, nc, body, 0, unroll=True)
        finish()

    @pl.when(kind == nc + 1)
    def _():
        o_ref[...] = jnp.zeros_like(o_ref)


def _expert_items(blk_expert, n_used, *, n_experts, nc):
    nb = blk_expert.shape[0]
    n_items = n_experts * nc + nb - n_experts
    j = jnp.arange(nb, dtype=I32)
    used = j < n_used
    prev = jnp.concatenate([blk_expert[:1] - 1, blk_expert[:-1]])
    first = used & (blk_expert != prev)
    per_block = jnp.where(first, nc, 1).astype(I32)
    ends = jnp.cumsum(per_block)
    starts = ends - per_block
    i = jnp.arange(n_items, dtype=I32)
    blk = jnp.minimum(jnp.searchsorted(ends, i, side="right").astype(I32), nb - 1)
    within = i - starts[blk]
    kind = jnp.where(i >= ends[nb - 1], nc + 2,
                     jnp.where(~used[blk], nc + 1, jnp.where(first[blk], within, nc))).astype(I32)
    exp = blk_expert[jnp.minimum(blk, n_used - 1)]
    wch = jnp.where(kind < nc, kind, nc - 1).astype(I32)
    return blk, kind, exp, wch


def _experts(blk_expert, n_used, buf, w1, b1, w2, b2, *, tm, tc):
    n_slots, half = buf.shape
    d = 2 * half
    ne, _, two_de = w1.shape
    de = two_de // 2
    nc = de // tc
    items = _expert_items(blk_expert, n_used, n_experts=ne, nc=nc)
    n_items = items[0].shape[0]
    return pl.pallas_call(
        functools.partial(_expert_kernel, nc=nc, tc=tc),
        grid_spec=pltpu.PrefetchScalarGridSpec(
            num_scalar_prefetch=4,
            grid=(n_items,),
            in_specs=[
                pl.BlockSpec((tm, half), lambda i, blk, kind, exp, wch: (blk[i], 0)),
                pl.BlockSpec((None, d, tc), lambda i, blk, kind, exp, wch: (exp[i], 0, wch[i])),
                pl.BlockSpec((None, d, tc), lambda i, blk, kind, exp, wch: (exp[i], 0, nc + wch[i])),
                pl.BlockSpec((None, 1, two_de), lambda i, blk, kind, exp, wch: (exp[i], 0, 0)),
                pl.BlockSpec((None, tc, d), lambda i, blk, kind, exp, wch: (exp[i], wch[i], 0)),
                pl.BlockSpec((None, 1, d), lambda i, blk, kind, exp, wch: (exp[i], 0, 0)),
            ],
            out_specs=pl.BlockSpec((tm, half), lambda i, blk, kind, exp, wch: (blk[i], 0)),
            scratch_shapes=[
                pltpu.VMEM((tm, d), BF16), pltpu.VMEM((tm, d), F32),
                pltpu.VMEM((nc, d, tc), BF16), pltpu.VMEM((nc, d, tc), BF16), pltpu.VMEM((nc, tc, d), BF16),
            ],
        ),
        out_shape=jax.ShapeDtypeStruct((n_slots, half), U32),
        compiler_params=_cparams(("arbitrary",)),
        name="moe_experts",
    )(*items, buf, w1, w1, b1, w2, b2)


def _combine_kernel(dest_sm, gate_ref, h_ref, g_ref, y_hbm, o_ref, rows, sem, *, t):
    i = pl.program_id(0)
    last = pl.num_programs(0) - 1
    cur = i % 2

    def row_copy(buf, slot, k, j):
        return pltpu.make_async_copy(y_hbm.at[pl.ds(slot, 1)], rows.at[buf, k, pl.ds(j, 1)], sem.at[buf])

    def gather(tile, buf):
        def issue(jb, c):
            j0 = pl.multiple_of(jb * ISSUE_GROUP, ISSUE_GROUP)
            for r in range(ISSUE_GROUP):
                for k in range(TOP_K):
                    row_copy(buf, dest_sm[(tile * t + j0 + r) * TOP_K + k], k, j0 + r).start(priority=k % 2)
            return c

        lax.fori_loop(0, t // ISSUE_GROUP, issue, 0)

    @pl.when(i == 0)
    def _():
        gather(0, 0)

    @pl.when(i < last)
    def _():
        gather(i + 1, 1 - cur)

    _drain(row_copy(cur, 0, 0, 0), t * TOP_K)
    gate = gate_ref[...]
    half = h_ref.shape[1] // 2
    y_lo = h_ref[:, :half]
    y_hi = h_ref[:, half:]
    for k in range(TOP_K):
        lo, hi = _unpack_bf16_pairs(rows[cur, k])
        y_lo = y_lo + gate[:, k:k + 1] * lo
        y_hi = y_hi + gate[:, k:k + 1] * hi
    o_ref[...] = _rms(jnp.concatenate([y_lo, y_hi], axis=1), g_ref[...])


def _combine(dest_flat, gate, h, g, ybuf, *, t):
    n, d = h.shape
    return pl.pallas_call(
        functools.partial(_combine_kernel, t=t),
        grid_spec=pltpu.PrefetchScalarGridSpec(
            num_scalar_prefetch=1,
            grid=(n // t,),
            in_specs=[
                pl.BlockSpec((t, LANES), lambda i, ds: (i, 0)),
                pl.BlockSpec((t, d), lambda i, ds: (i, 0)),
                pl.BlockSpec((1, d), lambda i, ds: (0, 0)),
                pl.BlockSpec(memory_space=pl.ANY),
            ],
            out_specs=pl.BlockSpec((t, d), lambda i, ds: (i, 0)),
            scratch_shapes=[pltpu.VMEM((2, TOP_K, t, d // 2), U32), pltpu.SemaphoreType.DMA((2,))],
        ),
        out_shape=jax.ShapeDtypeStruct((n, d), F32),
        compiler_params=_cparams(("arbitrary",)),
        name="moe_combine",
    )(dest_flat, gate, h, g, ybuf)


def _pad_lanes(v, lane0=0):
    out = jnp.zeros((1, LANES), F32)
    return out.at[0, lane0:lane0 + v.shape[0]].set(v.astype(F32))


def kernel(x, mem, ln_mix, w_in, conv_w, conv_b, dt_bias, a_log, d_skip, ssd_norm, fgate_bias, w_out, ln_xa, ln_mem,
           w_xq, w_xkv, w_xo, ln_ffn, w_router, b_router, w_moe1, b_moe1, w_moe2, b_moe2, ln_final):
    batch, seq, d = x.shape
    mem_len = mem.shape[1]
    depth = ln_mix.shape[0]
    n_ssd_heads = dt_bias.shape[1]
    n_fox_heads = fgate_bias.shape[1]
    d_ssd = n_ssd_heads * SSD_HEAD_DIM
    d_fox = n_fox_heads * FOX_HEAD_DIM
    cdim = d_ssd + 2 * SSD_GROUPS * SSD_STATE
    n_experts = w_router.shape[2]
    n = batch * seq
    fox_lane0 = n_ssd_heads
    assert n_ssd_heads + n_fox_heads <= LANES and cdim % d_ssd == 0 and d_ssd == d and d_fox == d

    tm = min(1024, n)
    o_z, o_xbc, o_dt = 0, d_ssd, d_ssd + cdim
    o_q = o_dt + n_ssd_heads
    o_k, o_v, o_f = o_q + d_fox, o_q + 2 * d_fox, o_q + 3 * d_fox
    q_col = (cdim + d_ssd) // FOX_HEAD_DIM
    k_col = q_col + n_fox_heads
    v_col = k_col + n_fox_heads

    head_of_ch = jnp.arange(d_ssd, dtype=I32) // SSD_HEAD_DIM
    e01 = (jnp.arange(LANES, dtype=I32)[:, None] == head_of_ch[None, :]).astype(BF16)

    n_assign = n * TOP_K
    nb = -(-n_assign // MOE_TM) + n_experts
    nb_pad = -(-nb // 8) * 8
    n_slots = nb * MOE_TM

    h = x.reshape(n, d)
    mem2 = mem.reshape(batch * mem_len, d)
    for i in range(depth):
        w = w_in[i]
        w_main = jnp.concatenate([w[:, o_xbc:o_dt], w[:, o_z:o_xbc], w[:, o_q:o_f]], axis=1).astype(BF16)
        w_small = jnp.concatenate(
            [w[:, o_dt:o_q], w[:, o_f:], jnp.zeros((d, LANES - n_ssd_heads - n_fox_heads), F32)], axis=1).astype(BF16)
        pmain, psmall = _inproj(h, ln_mix[i][None, :], w_main, w_small, tm, 1024)

        y_ssd, fcol, frow = _ssd(
            pmain, psmall, conv_w[i], conv_b[i][None, :], _pad_lanes(dt_bias[i]), _pad_lanes(a_log[i]),
            _pad_lanes(fgate_bias[i], fox_lane0), jnp.repeat(d_skip[i].astype(F32), SSD_HEAD_DIM)[None, :],
            ssd_norm[i][None, :], e01, batch=batch, seq=seq, d_inner=d_ssd, n_heads=n_ssd_heads, fox_lane0=fox_lane0)
        y_fox = _fox(pmain, fcol, frow, batch=batch, seq=seq, n_heads=n_fox_heads, q_col=q_col, k_col=k_col,
                     v_col=v_col, fox_lane0=fox_lane0, tq=min(256, seq))
        h = _mm_res([y_ssd, y_fox], w_out[i].astype(BF16), h, tm, 512, "out_proj")

        q = _norm_matmul(h, ln_xa[i][None, :], w_xq[i].astype(BF16), tm, 1024, "xa_q")
        kv = _norm_matmul(mem2, ln_mem[i][None, :], w_xkv[i].astype(BF16), min(1024, batch * mem_len), 1024, "xa_kv")
        o = _xattn(q, kv, batch=batch, seq=seq, mem_len=mem_len, d_model=d, tq=min(512, seq))
        h = _mm_res([o], w_xo[i].astype(BF16), h, tm, 512, "xa_o")

        wr = jnp.concatenate([w_router[i], jnp.zeros((d, LANES - n_experts), F32)], axis=1)
        idx, gate, cnt = _router(h, ln_ffn[i][None, :], wr, _pad_lanes(b_router[i]), n_experts=n_experts,
                                 tm=min(512, n))
        dest, be, meta = _slots(idx, cnt, n_experts=n_experts, tm_e=MOE_TM, nb_pad=nb_pad, t=min(512, n))
        dest_flat = dest[:, :TOP_K].reshape(-1)
        fill_lo = meta[0, :n_experts] + meta[1, :n_experts]
        fill_hi = meta[2, :n_experts]
        buf = _dispatch(dest_flat, fill_lo, fill_hi, h, ln_ffn[i][None, :], n_slots=n_slots, n_experts=n_experts,
                        t=min(MOE_ROW_TILE, n))
        ybuf = _experts(be[:nb, 0], meta[3, :1], buf, w_moe1[i], b_moe1[i][:, None, :], w_moe2[i],
                        b_moe2[i][:, None, :], tm=MOE_TM, tc=MOE_TC)
        last = i == depth - 1
        assert last, "the fused final norm assumes a single layer"
        h = _combine(dest_flat, gate, h, ln_final[None, :], ybuf, t=min(MOE_ROW_TILE, n))
    return h.reshape(batch, seq, d)
```

```python
import functools

import jax
import jax.numpy as jnp
from jax import lax
from jax.experimental import pallas as pl
from jax.experimental.pallas import tpu as pltpu

F32 = jnp.float32
BF16 = jnp.bfloat16
I32 = jnp.int32
U32 = jnp.uint32

RMS_EPS = 1e-5
LANES = 128
SSD_HEAD_DIM = 64
SSD_GROUPS = 8
SSD_STATE = 128
SSD_CONV = 4
SSD_CHUNK = 128
CONV_HALO = 8
FOX_HEAD_DIM = 128
XA_HEADS = 4
TOP_K = 4
SWIGLU_LIMIT = 7.0
SWIGLU_ALPHA = 1.702
NEG_BIG = -1e30
VMEM_LIMIT = 56 * 1024 * 1024

MOE_TM = 512
MOE_TC = 256
MOE_ROW_TILE = 512
DRAIN_UNROLL = 32
ISSUE_GROUP = 8


def _cparams(sem):
    return pltpu.CompilerParams(dimension_semantics=sem, vmem_limit_bytes=VMEM_LIMIT)


def _rms(x, g):
    ms = jnp.mean(x * x, axis=-1, keepdims=True)
    return x * lax.rsqrt(ms + RMS_EPS) * g


def _softplus(x):
    return jnp.maximum(x, 0.0) + jnp.log1p(jnp.exp(-jnp.abs(x)))


def _split3(x):
    hi = x.astype(BF16)
    r1 = x - hi.astype(F32)
    mid = r1.astype(BF16)
    lo = (r1 - mid.astype(F32)).astype(BF16)
    return hi, mid, lo


def _dot01_right(x, m01):
    hi, mid, lo = _split3(x)
    d = lambda a: jnp.dot(a, m01, preferred_element_type=F32)
    return d(hi) + d(mid) + d(lo)


def _dot01_left(m01, x):
    hi, mid, lo = _split3(x)
    d = lambda a: jnp.dot(m01, a, preferred_element_type=F32)
    return d(hi) + d(mid) + d(lo)


def _inproj_kernel(x_ref, g_ref, w_ref, ws_ref, o_ref, os_ref, u_scr):
    @pl.when(pl.program_id(1) == 0)
    def _():
        u = _rms(x_ref[...], g_ref[...]).astype(BF16)
        u_scr[...] = u
        os_ref[...] = jnp.dot(u, ws_ref[...], preferred_element_type=F32)

    o_ref[...] = jnp.dot(u_scr[...], w_ref[...], preferred_element_type=F32).astype(o_ref.dtype)


def _inproj(x, g, w_main, w_small, tm, tn):
    m, k = x.shape
    n = w_main.shape[1]
    return pl.pallas_call(
        _inproj_kernel,
        grid=(m // tm, n // tn),
        in_specs=[
            pl.BlockSpec((tm, k), lambda i, j: (i, 0)),
            pl.BlockSpec((1, k), lambda i, j: (0, 0)),
            pl.BlockSpec((k, tn), lambda i, j: (0, j)),
            pl.BlockSpec((k, LANES), lambda i, j: (0, 0)),
        ],
        out_specs=[
            pl.BlockSpec((tm, tn), lambda i, j: (i, j)),
            pl.BlockSpec((tm, LANES), lambda i, j: (i, 0)),
        ],
        out_shape=[jax.ShapeDtypeStruct((m, n), BF16), jax.ShapeDtypeStruct((m, LANES), F32)],
        scratch_shapes=[pltpu.VMEM((tm, k), BF16)],
        compiler_params=_cparams(("parallel", "arbitrary")),
        name="inproj",
    )(x, g, w_main, w_small)


def _norm_matmul_kernel(x_ref, g_ref, w_ref, o_ref, u_scr):
    @pl.when(pl.program_id(1) == 0)
    def _():
        u_scr[...] = _rms(x_ref[...], g_ref[...]).astype(BF16)

    o_ref[...] = jnp.dot(u_scr[...], w_ref[...], preferred_element_type=F32).astype(o_ref.dtype)


def _norm_matmul(x, g, w, tm, tn, name):
    m, k = x.shape
    n = w.shape[1]
    return pl.pallas_call(
        _norm_matmul_kernel,
        grid=(m // tm, n // tn),
        in_specs=[
            pl.BlockSpec((tm, k), lambda i, j: (i, 0)),
            pl.BlockSpec((1, k), lambda i, j: (0, 0)),
            pl.BlockSpec((k, tn), lambda i, j: (0, j)),
        ],
        out_specs=pl.BlockSpec((tm, tn), lambda i, j: (i, j)),
        out_shape=jax.ShapeDtypeStruct((m, n), BF16),
        scratch_shapes=[pltpu.VMEM((tm, k), BF16)],
        compiler_params=_cparams(("parallel", "arbitrary")),
        name=name,
    )(x, g, w)


def _mm_res_kernel(*refs, n_lhs):
    res_ref, o_ref = refs[2 * n_lhs], refs[2 * n_lhs + 1]
    acc = res_ref[...]
    for a_ref, w_ref in zip(refs[:n_lhs], refs[n_lhs:2 * n_lhs]):
        acc = acc + jnp.dot(a_ref[...], w_ref[...], preferred_element_type=F32)
    o_ref[...] = acc


def _mm_res(lhs, w, res, tm, tn, name):
    m, n = res.shape
    n_lhs = len(lhs)
    kb = lhs[0].shape[1]
    assert all(a.shape[1] == kb for a in lhs) and w.shape[0] == n_lhs * kb
    ws = [w] * n_lhs
    in_specs = [pl.BlockSpec((tm, kb), lambda i, j: (i, 0)) for _ in lhs]
    in_specs += [pl.BlockSpec((kb, tn), lambda i, j, r=r: (r, j)) for r in range(n_lhs)]
    in_specs += [pl.BlockSpec((tm, tn), lambda i, j: (i, j))]
    return pl.pallas_call(
        functools.partial(_mm_res_kernel, n_lhs=n_lhs),
        grid=(m // tm, n // tn),
        in_specs=in_specs,
        out_specs=pl.BlockSpec((tm, tn), lambda i, j: (i, j)),
        out_shape=jax.ShapeDtypeStruct((m, n), F32),
        compiler_params=_cparams(("parallel", "arbitrary")),
        name=name,
    )(*lhs, *ws, res)


def _ssd_kernel(xbc_ref, z_ref, sm_ref, cw_ref, cb_ref, dtb_ref, alog_ref, fb_ref, dch_ref, gn_ref, e_ref,
                y_ref, fcol_ref, frow_ref,
                ext, cv, state, fcarry, yacc, dtch, csch, *, d_inner, n_heads, fox_lane0):
    L = SSD_CHUNK
    gw = d_inner // SSD_GROUPS
    hpg = gw // SSD_HEAD_DIM
    cdim = d_inner + 2 * SSD_GROUPS * SSD_STATE
    strip = 256

    @pl.when(pl.program_id(1) == 0)
    def _():
        ext[...] = jnp.zeros_like(ext)
        state[...] = jnp.zeros_like(state)
        fcarry[...] = jnp.zeros_like(fcarry)

    row = lax.broadcasted_iota(I32, (L, L), 0)
    col = lax.broadcasted_iota(I32, (L, L), 1)
    causal = col <= row
    tri = causal.astype(BF16)

    shifts = [(col == row - (SSD_CONV - 1 - k)).astype(BF16) for k in range(SSD_CONV - 1)]
    for s0 in range(0, cdim, strip):
        cur = xbc_ref[:, s0:s0 + strip]
        acc = cb_ref[:, s0:s0 + strip] + cw_ref[SSD_CONV - 1:SSD_CONV, s0:s0 + strip] * cur.astype(F32)
        head = jnp.zeros((CONV_HALO, strip), F32)
        for k in range(SSD_CONV - 1):
            wk = cw_ref[k:k + 1, s0:s0 + strip]
            acc = acc + wk * jnp.dot(shifts[k], cur, preferred_element_type=F32)
            r0 = CONV_HALO - (SSD_CONV - 1) + k
            head = head + wk * ext[r0:r0 + CONV_HALO, s0:s0 + strip]
        acc = jnp.concatenate([acc[0:CONV_HALO] + head, acc[CONV_HALO:]], axis=0)
        cv[:, s0:s0 + strip] = acc * jax.nn.sigmoid(acc)
    ext[0:CONV_HALO, :] = xbc_ref[L - CONV_HALO:L, :].astype(F32)

    sm = sm_ref[...]
    dt = _softplus(sm + dtb_ref[...])
    dta = dt * (-jnp.exp(alog_ref[...]))
    cs = _dot01_left(tri, dta)
    cst = cs.T
    logf = -_softplus(-(sm + fb_ref[...]))
    fc = _dot01_left(tri, logf) + fcarry[...]
    fcol_ref[...] = fc
    frow_ref[...] = fc.T
    fcarry[...] = fc[L - 1:L, :]

    e01 = e_ref[...]
    dtch[...] = _dot01_right(dt, e01)
    csch[...] = _dot01_right(cs, e01)

    lane_g = lax.broadcasted_iota(I32, (1, gw), 1)
    for g in range(SSD_GROUPS):
        c0 = g * gw
        bg = cv[:, d_inner + g * SSD_STATE:d_inner + (g + 1) * SSD_STATE]
        cg = cv[:, d_inner + SSD_GROUPS * SSD_STATE + g * SSD_STATE:d_inner + SSD_GROUPS * SSD_STATE + (g + 1) * SSD_STATE]
        cgb = cg.astype(BF16)
        cbm = lax.dot_general(cgb, bg.astype(BF16), (((1,), (1,)), ((), ())), preferred_element_type=F32)
        bgt = bg.T.astype(BF16)
        xg = cv[:, c0:c0 + gw]
        csg = csch[:, c0:c0 + gw]
        cs_last = csg[L - 1:L, :]
        xdt = xg * dtch[:, c0:c0 + gw]
        xw = (xdt * jnp.exp(cs_last - csg)).astype(BF16)
        st_new = jnp.dot(bgt, xw, preferred_element_type=F32)
        prev = state[g]
        yg = jnp.dot(cgb, prev.astype(BF16), preferred_element_type=F32) * jnp.exp(csg)
        state[g] = prev * jnp.exp(cs_last) + st_new
        yg = yg + xg * dch_ref[:, c0:c0 + gw]
        for r in range(hpg):
            h = g * hpg + r
            seg = cs[:, h:h + 1] - cst[h:h + 1, :]
            m = (cbm * jnp.exp(jnp.where(causal, seg, NEG_BIG))).astype(BF16)
            head = (lane_g >= r * SSD_HEAD_DIM) & (lane_g < (r + 1) * SSD_HEAD_DIM)
            xh = jnp.where(head, xdt, 0.0).astype(BF16)
            yg = yg + jnp.dot(m, xh, preferred_element_type=F32)
        yacc[:, c0:c0 + gw] = yg

    z = z_ref[...].astype(F32)
    gated = yacc[...] * (z * jax.nn.sigmoid(z))
    y_ref[...] = _rms(gated, gn_ref[...]).astype(y_ref.dtype)


def _ssd(pmain, psmall, conv_w, conv_b, dtb, alog, fb, dch, gnorm, e01, *, batch, seq, d_inner, n_heads, fox_lane0):
    nc = seq // SSD_CHUNK
    cdim = d_inner + 2 * SSD_GROUPS * SSD_STATE
    gw = d_inner // SSD_GROUPS
    m = batch * seq
    rowblk = lambda b, c: b * nc + c
    const = lambda shape: pl.BlockSpec(shape, lambda b, c: (0, 0))
    kern = functools.partial(_ssd_kernel, d_inner=d_inner, n_heads=n_heads, fox_lane0=fox_lane0)
    return pl.pallas_call(
        kern,
        grid=(batch, nc),
        in_specs=[
            pl.BlockSpec((SSD_CHUNK, cdim), lambda b, c: (rowblk(b, c), 0)),
            pl.BlockSpec((SSD_CHUNK, d_inner), lambda b, c: (rowblk(b, c), cdim // d_inner)),
            pl.BlockSpec((SSD_CHUNK, LANES), lambda b, c: (rowblk(b, c), 0)),
            const((SSD_CONV, cdim)), const((1, cdim)), const((1, LANES)), const((1, LANES)), const((1, LANES)),
            const((1, d_inner)), const((1, d_inner)), const((LANES, d_inner)),
        ],
        out_specs=[
            pl.BlockSpec((SSD_CHUNK, d_inner), lambda b, c: (rowblk(b, c), 0)),
            pl.BlockSpec((SSD_CHUNK, LANES), lambda b, c: (rowblk(b, c), 0)),
            pl.BlockSpec((None, LANES, SSD_CHUNK), lambda b, c: (b, 0, c)),
        ],
        out_shape=[
            jax.ShapeDtypeStruct((m, d_inner), BF16),
            jax.ShapeDtypeStruct((m, LANES), F32),
            jax.ShapeDtypeStruct((batch, LANES, seq), F32),
        ],
        scratch_shapes=[
            pltpu.VMEM((2 * CONV_HALO, cdim), F32),
            pltpu.VMEM((SSD_CHUNK, cdim), F32),
            pltpu.VMEM((SSD_GROUPS, SSD_STATE, gw), F32),
            pltpu.VMEM((1, LANES), F32),
            pltpu.VMEM((SSD_CHUNK, d_inner), F32),
            pltpu.VMEM((SSD_CHUNK, d_inner), F32),
            pltpu.VMEM((SSD_CHUNK, d_inner), F32),
        ],
        compiler_params=_cparams(("arbitrary", "arbitrary")),
        name="ssd_scan",
    )(pmain, pmain, psmall, conv_w, conv_b, dtb, alog, fb, dch, gnorm, e01)


def _fox_kernel(q_ref, k_ref, v_ref, fcol_ref, frow_ref, o_ref, s_scr, fq_scr, m_scr, l_scr, acc_scr, *, tq, fox_lane0):
    h = pl.program_id(1)
    nq = q_ref.shape[0] // tq
    ng = tq // LANES
    scale = FOX_HEAD_DIM ** -0.5
    lane = lax.broadcasted_iota(I32, (tq, LANES), 1)
    causal = lax.broadcasted_iota(I32, (tq, tq), 1) <= lax.broadcasted_iota(I32, (tq, tq), 0)
    groups = lambda a: [a[:, g * LANES:(g + 1) * LANES] for g in range(ng)]

    for qi in range(nq):
        qs = qi * tq
        q = (q_ref[qs:qs + tq, :].astype(F32) * scale).astype(BF16)
        fq = jnp.sum(jnp.where(lane == fox_lane0 + h, fcol_ref[qs:qs + tq, :], 0.0), axis=-1, keepdims=True)
        fq_scr[...] = jnp.broadcast_to(fq, (tq, LANES))
        m_scr[...] = jnp.full((tq, LANES), NEG_BIG, F32)
        for kb in range(qi + 1):
            ks = kb * tq
            fk = frow_ref[pl.ds(fox_lane0 + h, 1), ks:ks + tq]
            s = lax.dot_general(q, k_ref[ks:ks + tq, :], (((1,), (1,)), ((), ())), preferred_element_type=F32)
            fqr = fq_scr[...]
            s = jnp.concatenate([sg + fqr for sg in groups(s)], axis=1) - fk
            if kb == qi:
                s = jnp.where(causal, s, NEG_BIG)
            s_scr[kb] = s
            m_scr[...] = functools.reduce(jnp.maximum, groups(s), m_scr[...])
        m_row = jnp.max(m_scr[...], axis=-1, keepdims=True)
        m_scr[...] = jnp.broadcast_to(m_row, (tq, LANES))
        l_scr[...] = jnp.zeros_like(l_scr)
        acc_scr[...] = jnp.zeros_like(acc_scr)
        for kb in range(qi + 1):
            ks = kb * tq
            mr = m_scr[...]
            ps = [jnp.exp(sg - mr) for sg in groups(s_scr[kb])]
            l_scr[...] += functools.reduce(jnp.add, ps)
            p = jnp.concatenate(ps, axis=1).astype(BF16)
            acc_scr[...] += jnp.dot(p, v_ref[ks:ks + tq, :], preferred_element_type=F32)
        l = jnp.sum(l_scr[...], axis=-1, keepdims=True)
        o_ref[qs:qs + tq, :] = (acc_scr[...] * (1.0 / l)).astype(o_ref.dtype)


def _fox(pmain, fcol, frow, *, batch, seq, n_heads, q_col, k_col, v_col, fox_lane0, tq):
    kern = functools.partial(_fox_kernel, tq=tq, fox_lane0=fox_lane0)
    head_block = lambda col0: pl.BlockSpec((seq, FOX_HEAD_DIM), lambda b, h: (b, col0 + h))
    return pl.pallas_call(
        kern,
        grid=(batch, n_heads),
        in_specs=[
            head_block(q_col), head_block(k_col), head_block(v_col),
            pl.BlockSpec((seq, LANES), lambda b, h: (b, 0)),
            pl.BlockSpec((None, LANES, seq), lambda b, h: (b, 0, 0)),
        ],
        out_specs=head_block(0),
        out_shape=jax.ShapeDtypeStruct((batch * seq, n_heads * FOX_HEAD_DIM), BF16),
        scratch_shapes=[
            pltpu.VMEM((seq // tq, tq, tq), F32),
            pltpu.VMEM((tq, LANES), F32),
            pltpu.VMEM((tq, LANES), F32),
            pltpu.VMEM((tq, LANES), F32),
            pltpu.VMEM((tq, FOX_HEAD_DIM), F32),
        ],
        compiler_params=_cparams(("parallel", "arbitrary")),
        name="fox_attn",
    )(pmain, pmain, pmain, fcol, frow)


def _xattn_kernel(q_ref, kv_ref, o_ref, *, d_model):
    hd = d_model // XA_HEADS
    scale = hd ** -0.5
    for h in range(XA_HEADS):
        q = q_ref[:, h * hd:(h + 1) * hd]
        k = kv_ref[:, h * hd:(h + 1) * hd]
        v = kv_ref[:, d_model + h * hd:d_model + (h + 1) * hd]
        s = lax.dot_general(q, k, (((1,), (1,)), ((), ())), preferred_element_type=F32) * scale
        p = jnp.exp(s - jnp.max(s, axis=-1, keepdims=True))
        p = p * (1.0 / jnp.sum(p, axis=-1, keepdims=True))
        o_ref[:, h * hd:(h + 1) * hd] = jnp.dot(p.astype(BF16), v, preferred_element_type=F32).astype(o_ref.dtype)


def _xattn(q, kv, *, batch, seq, mem_len, d_model, tq):
    nq = seq // tq
    return pl.pallas_call(
        functools.partial(_xattn_kernel, d_model=d_model),
        grid=(batch, nq),
        in_specs=[
            pl.BlockSpec((tq, d_model), lambda b, i: (b * nq + i, 0)),
            pl.BlockSpec((mem_len, 2 * d_model), lambda b, i: (b, 0)),
        ],
        out_specs=pl.BlockSpec((tq, d_model), lambda b, i: (b * nq + i, 0)),
        out_shape=jax.ShapeDtypeStruct((batch * seq, d_model), BF16),
        compiler_params=_cparams(("parallel", "arbitrary")),
        name="mem_xattn",
    )(q, kv)


def _router_kernel(h_ref, g_ref, wr_ref, br_ref, idx_ref, gate_ref, cnt_ref, *, n_experts):
    tm = h_ref.shape[0]

    @pl.when(pl.program_id(0) == 0)
    def _():
        cnt_ref[...] = jnp.zeros_like(cnt_ref)

    u = _rms(h_ref[...], g_ref[...])
    u_hi, u_mid, _ = _split3(u)
    w_hi, w_mid, _ = _split3(wr_ref[...])
    d = lambda a, b: jnp.dot(a, b, preferred_element_type=F32)
    logits = d(u_hi, w_hi) + (d(u_hi, w_mid) + d(u_mid, w_hi)) + br_ref[...]
    lane = lax.broadcasted_iota(I32, (tm, LANES), 1)
    work = jnp.where(lane < n_experts, logits, -jnp.inf)
    vals, idxs = [], []
    for _ in range(TOP_K):
        mx = jnp.max(work, axis=-1, keepdims=True)
        am = jnp.min(jnp.where(work == mx, lane, LANES), axis=-1, keepdims=True)
        vals.append(mx)
        idxs.append(am)
        work = jnp.where(lane == am, -jnp.inf, work)
    ex = [jnp.exp(v - vals[0]) for v in vals]
    den = ex[0] + ex[1] + ex[2] + ex[3]
    idx_out = jnp.zeros((tm, LANES), I32)
    gate_out = jnp.zeros((tm, LANES), F32)
    hits = jnp.zeros((tm, LANES), F32)
    for k in range(TOP_K):
        idx_out = jnp.where(lane == k, idxs[k], idx_out)
        gate_out = jnp.where(lane == k, ex[k] / den, gate_out)
        hits = hits + (lane == idxs[k]).astype(F32)
    idx_ref[...] = idx_out
    gate_ref[...] = gate_out
    cnt_ref[...] += jnp.broadcast_to(jnp.sum(hits, axis=0, keepdims=True), cnt_ref.shape)


def _router(h, g, wr, br, *, n_experts, tm):
    n, d = h.shape
    return pl.pallas_call(
        functools.partial(_router_kernel, n_experts=n_experts),
        grid=(n // tm,),
        in_specs=[
            pl.BlockSpec((tm, d), lambda i: (i, 0)),
            pl.BlockSpec((1, d), lambda i: (0, 0)),
            pl.BlockSpec((d, LANES), lambda i: (0, 0)),
            pl.BlockSpec((1, LANES), lambda i: (0, 0)),
        ],
        out_specs=[
            pl.BlockSpec((tm, LANES), lambda i: (i, 0)),
            pl.BlockSpec((tm, LANES), lambda i: (i, 0)),
            pl.BlockSpec((8, LANES), lambda i: (0, 0)),
        ],
        out_shape=[
            jax.ShapeDtypeStruct((n, LANES), I32),
            jax.ShapeDtypeStruct((n, LANES), F32),
            jax.ShapeDtypeStruct((8, LANES), F32),
        ],
        compiler_params=_cparams(("arbitrary",)),
        name="moe_router",
    )(h, g, wr, br)


def _slots_kernel(idx_ref, cnt_ref, dest_ref, be_ref, meta_ref, carry, *, n_experts, tm_e, nb_pad):
    t = idx_ref.shape[0]
    lane1 = lax.broadcasted_iota(I32, (1, LANES), 1)

    @pl.when(pl.program_id(0) == 0)
    def _():
        cnt = cnt_ref[0:1, :]
        padded = jnp.floor((cnt + (tm_e - 1)) / tm_e) * tm_e
        er = lax.broadcasted_iota(I32, (LANES, LANES), 0)
        ec = lax.broadcasted_iota(I32, (LANES, LANES), 1)
        start = _dot01_right(jnp.broadcast_to(padded, (8, LANES)), (er < ec).astype(BF16))[0:1, :]
        carry[...] = start
        pend = start + padded
        jrow = (lax.broadcasted_iota(I32, (nb_pad, LANES), 0) * tm_e).astype(F32)
        lane = lax.broadcasted_iota(I32, (nb_pad, LANES), 1)
        hit = jnp.where((pend <= jrow) & (lane < n_experts), 1.0, 0.0)
        be = jnp.minimum(jnp.sum(hit, axis=-1, keepdims=True), n_experts - 1.0)
        be_ref[...] = jnp.broadcast_to(be, (nb_pad, LANES)).astype(I32)
        total = jnp.sum(jnp.where(lane1 < n_experts, padded, 0.0), axis=-1, keepdims=True)
        meta = jnp.concatenate([start, cnt, pend, jnp.broadcast_to(total / tm_e, (1, LANES)),
                                jnp.zeros((4, LANES), F32)], axis=0)
        meta_ref[...] = meta.astype(I32)

    idx = idx_ref[...]
    lane = lax.broadcasted_iota(I32, (t, LANES), 1)
    ohs = [lane == idx[:, k:k + 1] for k in range(TOP_K)]
    oh = jnp.zeros((t, LANES), F32)
    for o in ohs:
        oh = oh + o.astype(F32)
    r = lax.broadcasted_iota(I32, (t, t), 0)
    c = lax.broadcasted_iota(I32, (t, t), 1)
    earlier = jnp.dot((c < r).astype(BF16), oh.astype(BF16), preferred_element_type=F32)
    base = carry[...] + earlier
    dest = jnp.zeros((t, LANES), F32)
    for k in range(TOP_K):
        dk = jnp.sum(jnp.where(ohs[k], base, 0.0), axis=-1, keepdims=True)
        dest = jnp.where(lane == k, dk, dest)
    dest_ref[...] = dest.astype(I32)
    carry[...] += jnp.sum(oh, axis=0, keepdims=True)


def _slots(idx, cnt, *, n_experts, tm_e, nb_pad, t):
    n = idx.shape[0]
    return pl.pallas_call(
        functools.partial(_slots_kernel, n_experts=n_experts, tm_e=tm_e, nb_pad=nb_pad),
        grid=(n // t,),
        in_specs=[pl.BlockSpec((t, LANES), lambda i: (i, 0)), pl.BlockSpec((8, LANES), lambda i: (0, 0))],
        out_specs=[
            pl.BlockSpec((t, LANES), lambda i: (i, 0)),
            pl.BlockSpec((nb_pad, LANES), lambda i: (0, 0)),
            pl.BlockSpec((8, LANES), lambda i: (0, 0)),
        ],
        out_shape=[
            jax.ShapeDtypeStruct((n, LANES), I32),
            jax.ShapeDtypeStruct((nb_pad, LANES), I32),
            jax.ShapeDtypeStruct((8, LANES), I32),
        ],
        scratch_shapes=[pltpu.VMEM((1, LANES), F32)],
        compiler_params=_cparams(("arbitrary",)),
        name="moe_slots",
    )(idx, cnt)


def _drain(copy, n):
    assert n % DRAIN_UNROLL == 0

    def body(_, c):
        for _ in range(DRAIN_UNROLL):
            copy.wait()
        return c

    lax.fori_loop(0, n // DRAIN_UNROLL, body, 0)


def _dispatch_kernel(dest_sm, fill_lo_sm, fill_hi_sm, h_ref, g_ref, buf_hbm, urows, zrow, sem, zsem, *, t, n_experts):
    i = pl.program_id(0)
    last = pl.num_programs(0) - 1
    cur = i % 2

    def row_copy(buf, j, slot):
        return pltpu.make_async_copy(urows.at[buf, pl.ds(j, 1)], buf_hbm.at[pl.ds(slot, 1)], sem.at[buf])

    def zero_copy(slot):
        return pltpu.make_async_copy(zrow.at[pl.ds(0, 1)], buf_hbm.at[pl.ds(slot, 1)], zsem)

    @pl.when(i == 0)
    def _():
        zrow[...] = jnp.zeros_like(zrow)
        for e in range(n_experts):
            lo, hi = fill_lo_sm[e], fill_hi_sm[e]

            def zstart(s, c):
                zero_copy(s).start()
                return c

            def zwait(s, c):
                zero_copy(s).wait()
                return c

            lax.fori_loop(lo, hi, zstart, 0)
            lax.fori_loop(lo, hi, zwait, 0)

        def tail_copy(b):
            s = pl.multiple_of(fill_hi_sm[n_experts - 1] + b * zrow.shape[0], zrow.shape[0])
            return pltpu.make_async_copy(zrow, buf_hbm.at[pl.ds(s, zrow.shape[0])], zsem)

        n_tail = (buf_hbm.shape[0] - fill_hi_sm[n_experts - 1]) // zrow.shape[0]

        def tstart(b, c):
            tail_copy(b).start()
            return c

        def twait(b, c):
            tail_copy(b).wait()
            return c

        lax.fori_loop(0, n_tail, tstart, 0)
        lax.fori_loop(0, n_tail, twait, 0)

    base = i * t
    urows[cur] = _pack_bf16_pairs(_rms(h_ref[...], g_ref[...]))

    def issue(jb, c):
        j0 = pl.multiple_of(jb * ISSUE_GROUP, ISSUE_GROUP)
        for r in range(ISSUE_GROUP):
            for k in range(TOP_K):
                row_copy(cur, j0 + r, dest_sm[(base + j0 + r) * TOP_K + k]).start(priority=k % 2)
        return c

    lax.fori_loop(0, t // ISSUE_GROUP, issue, 0)

    @pl.when(i > 0)
    def _():
        _drain(row_copy(1 - cur, 0, 0), t * TOP_K)

    @pl.when(i == last)
    def _():
        _drain(row_copy(cur, 0, 0), t * TOP_K)


def _dispatch(dest_flat, fill_lo, fill_hi, h, g, *, n_slots, n_experts, t):
    n, d = h.shape
    return pl.pallas_call(
        functools.partial(_dispatch_kernel, t=t, n_experts=n_experts),
        grid_spec=pltpu.PrefetchScalarGridSpec(
            num_scalar_prefetch=3,
            grid=(n // t,),
            in_specs=[pl.BlockSpec((t, d), lambda i, *_: (i, 0)), pl.BlockSpec((1, d), lambda i, *_: (0, 0))],
            out_specs=pl.BlockSpec(memory_space=pl.ANY),
            scratch_shapes=[pltpu.VMEM((2, t, d // 2), U32), pltpu.VMEM((8, d // 2), U32),
                            pltpu.SemaphoreType.DMA((2,)), pltpu.SemaphoreType.DMA(())],
        ),
        out_shape=jax.ShapeDtypeStruct((n_slots, d // 2), U32),
        compiler_params=_cparams(("arbitrary",)),
        name="moe_dispatch",
    )(dest_flat, fill_lo, fill_hi, h, g)


def _pack_bf16_pairs(x):
    w = x.shape[1] // 2
    bits = lambda a: lax.bitcast_convert_type(a.astype(BF16).astype(F32), U32)
    return (bits(x[:, :w]) >> 16) | (bits(x[:, w:]) & jnp.uint32(0xFFFF0000))


def _unpack_bf16_pairs(p):
    lo = lax.bitcast_convert_type(p << 16, F32)
    hi = lax.bitcast_convert_type(p & jnp.uint32(0xFFFF0000), F32)
    return lo, hi


def _expert_kernel(blk_sm, kind_sm, exp_sm, wch_sm, x_ref, w1g_ref, w1u_ref, b1_ref, w2_ref, b2_ref, o_ref,
                   xb, acc, w1g_res, w1u_res, w2_res, *, nc, tc):
    kind = kind_sm[pl.program_id(0)]
    half = xb.shape[1] // 2
    de = nc * tc

    def begin():
        lo, hi = _unpack_bf16_pairs(x_ref[...])
        xb[:, :half] = lo.astype(BF16)
        xb[:, half:] = hi.astype(BF16)
        acc[...] = jnp.broadcast_to(b2_ref[...], acc.shape)

    def chunk(c):
        x = xb[...]
        c0 = pl.multiple_of(c * tc, tc)
        g = jnp.dot(x, w1g_res[c], preferred_element_type=F32) + b1_ref[:, pl.ds(c0, tc)]
        u = jnp.dot(x, w1u_res[c], preferred_element_type=F32) + b1_ref[:, pl.ds(de + c0, tc)]
        g = jnp.minimum(g, SWIGLU_LIMIT)
        u = jnp.clip(u, -SWIGLU_LIMIT, SWIGLU_LIMIT)
        act = g * jax.nn.sigmoid(SWIGLU_ALPHA * g) * (u + 1.0)
        acc[...] += jnp.dot(act.astype(BF16), w2_res[c], preferred_element_type=F32)

    def finish():
        o_ref[...] = _pack_bf16_pairs(acc[...])

    @pl.when(kind < nc)
    def _():
        w1g_res[kind] = w1g_ref[...].astype(BF16)
        w1u_res[kind] = w1u_ref[...].astype(BF16)
        w2_res[kind] = w2_ref[...].astype(BF16)
        pl.when(kind == 0)(begin)
        chunk(kind)
        pl.when(kind == nc - 1)(finish)

    @pl.when(kind == nc)
    def _():
        lo, hi = _unpack_bf16_pairs(x_ref[...])
        x = jnp.concatenate([lo.astype(BF16), hi.astype(BF16)], axis=1)
        acts = []
        for c in range(nc):
            g = jnp.dot(x, w1g_res[c], preferred_element_type=F32) + b1_ref[:, c * tc:(c + 1) * tc]
            u = jnp.dot(x, w1u_res[c], preferred_element_type=F32) + b1_ref[:, de + c * tc:de + (c + 1) * tc]
            g = jnp.minimum(g, SWIGLU_LIMIT)
            u = jnp.clip(u, -SWIGLU_LIMIT, SWIGLU_LIMIT)
            acts.append((g * jax.nn.sigmoid(SWIGLU_ALPHA * g) * (u + 1.0)).astype(BF16))
        act = jnp.concatenate(acts, axis=1)
        w2_all = w2_res[...].reshape(de, w2_res.shape[2])
        y = jnp.dot(act, w2_all, preferred_element_type=F32) + b2_ref[...]
        o_ref[...] = _pack_bf16_pairs(y)

    @pl.when(kind == nc + 1)
    def _():
        o_ref[...] = jnp.zeros_like(o_ref)


def _expert_items(blk_expert, n_used, *, n_experts, nc):
    nb = blk_expert.shape[0]
    n_items = n_experts * nc + nb - n_experts
    j = jnp.arange(nb, dtype=I32)
    used = j < n_used
    prev = jnp.concatenate([blk_expert[:1] - 1, blk_expert[:-1]])
    first = used & (blk_expert != prev)
    per_block = jnp.where(first, nc, 1).astype(I32)
    ends = jnp.cumsum(per_block)
    starts = ends - per_block
    i = jnp.arange(n_items, dtype=I32)
    blk = jnp.minimum(jnp.searchsorted(ends, i, side="right").astype(I32), nb - 1)
    within = i - starts[blk]
    kind = jnp.where(i >= ends[nb - 1], nc + 2,
                     jnp.where(~used[blk], nc + 1, jnp.where(first[blk], within, nc))).astype(I32)
    exp = blk_expert[jnp.minimum(blk, n_used - 1)]
    wch = jnp.where(kind < nc, kind, nc - 1).astype(I32)
    return blk, kind, exp, wch


def _experts(blk_expert, n_used, buf, w1, b1, w2, b2, *, tm, tc):
    n_slots, half = buf.shape
    d = 2 * half
    ne, _, two_de = w1.shape
    de = two_de // 2
    nc = de // tc
    items = _expert_items(blk_expert, n_used, n_experts=ne, nc=nc)
    n_items = items[0].shape[0]
    return pl.pallas_call(
        functools.partial(_expert_kernel, nc=nc, tc=tc),
        grid_spec=pltpu.PrefetchScalarGridSpec(
            num_scalar_prefetch=4,
            grid=(n_items,),
            in_specs=[
                pl.BlockSpec((tm, half), lambda i, blk, kind, exp, wch: (blk[i], 0)),
                pl.BlockSpec((None, d, tc), lambda i, blk, kind, exp, wch: (exp[i], 0, wch[i])),
                pl.BlockSpec((None, d, tc), lambda i, blk, kind, exp, wch: (exp[i], 0, nc + wch[i])),
                pl.BlockSpec((None, 1, two_de), lambda i, blk, kind, exp, wch: (exp[i], 0, 0)),
                pl.BlockSpec((None, tc, d), lambda i, blk, kind, exp, wch: (exp[i], wch[i], 0)),
                pl.BlockSpec((None, 1, d), lambda i, blk, kind, exp, wch: (exp[i], 0, 0)),
            ],
            out_specs=pl.BlockSpec((tm, half), lambda i, blk, kind, exp, wch: (blk[i], 0)),
            scratch_shapes=[
                pltpu.VMEM((tm, d), BF16), pltpu.VMEM((tm, d), F32),
                pltpu.VMEM((nc, d, tc), BF16), pltpu.VMEM((nc, d, tc), BF16), pltpu.VMEM((nc, tc, d), BF16),
            ],
        ),
        out_shape=jax.ShapeDtypeStruct((n_slots, half), U32),
        compiler_params=_cparams(("arbitrary",)),
        name="moe_experts",
    )(*items, buf, w1, w1, b1, w2, b2)


def _combine_kernel(dest_sm, gate_ref, h_ref, g_ref, y_hbm, o_ref, rows, sem, *, t):
    i = pl.program_id(0)
    last = pl.num_programs(0) - 1
    cur = i % 2

    def row_copy(buf, slot, k, j):
        return pltpu.make_async_copy(y_hbm.at[pl.ds(slot, 1)], rows.at[buf, k, pl.ds(j, 1)], sem.at[buf])

    def gather(tile, buf):
        def issue(jb, c):
            j0 = pl.multiple_of(jb * ISSUE_GROUP, ISSUE_GROUP)
            for r in range(ISSUE_GROUP):
                for k in range(TOP_K):
                    row_copy(buf, dest_sm[(tile * t + j0 + r) * TOP_K + k], k, j0 + r).start(priority=k % 2)
            return c

        lax.fori_loop(0, t // ISSUE_GROUP, issue, 0)

    @pl.when(i == 0)
    def _():
        gather(0, 0)

    @pl.when(i < last)
    def _():
        gather(i + 1, 1 - cur)

    _drain(row_copy(cur, 0, 0, 0), t * TOP_K)
    gate = gate_ref[...]
    half = h_ref.shape[1] // 2
    y_lo = h_ref[:, :half]
    y_hi = h_ref[:, half:]
    for k in range(TOP_K):
        lo, hi = _unpack_bf16_pairs(rows[cur, k])
        y_lo = y_lo + gate[:, k:k + 1] * lo
        y_hi = y_hi + gate[:, k:k + 1] * hi
    o_ref[...] = _rms(jnp.concatenate([y_lo, y_hi], axis=1), g_ref[...])


def _combine(dest_flat, gate, h, g, ybuf, *, t):
    n, d = h.shape
    return pl.pallas_call(
        functools.partial(_combine_kernel, t=t),
        grid_spec=pltpu.PrefetchScalarGridSpec(
            num_scalar_prefetch=1,
            grid=(n // t,),
            in_specs=[
                pl.BlockSpec((t, LANES), lambda i, ds: (i, 0)),
                pl.BlockSpec((t, d), lambda i, ds: (i, 0)),
                pl.BlockSpec((1, d), lambda i, ds: (0, 0)),
                pl.BlockSpec(memory_space=pl.ANY),
            ],
            out_specs=pl.BlockSpec((t, d), lambda i, ds: (i, 0)),
            scratch_shapes=[pltpu.VMEM((2, TOP_K, t, d // 2), U32), pltpu.SemaphoreType.DMA((2,))],
        ),
        out_shape=jax.ShapeDtypeStruct((n, d), F32),
        compiler_params=_cparams(("arbitrary",)),
        name="moe_combine",
    )(dest_flat, gate, h, g, ybuf)


def _pad_lanes(v, lane0=0):
    out = jnp.zeros((1, LANES), F32)
    return out.at[0, lane0:lane0 + v.shape[0]].set(v.astype(F32))


def kernel(x, mem, ln_mix, w_in, conv_w, conv_b, dt_bias, a_log, d_skip, ssd_norm, fgate_bias, w_out, ln_xa, ln_mem,
           w_xq, w_xkv, w_xo, ln_ffn, w_router, b_router, w_moe1, b_moe1, w_moe2, b_moe2, ln_final):
    batch, seq, d = x.shape
    mem_len = mem.shape[1]
    depth = ln_mix.shape[0]
    n_ssd_heads = dt_bias.shape[1]
    n_fox_heads = fgate_bias.shape[1]
    d_ssd = n_ssd_heads * SSD_HEAD_DIM
    d_fox = n_fox_heads * FOX_HEAD_DIM
    cdim = d_ssd + 2 * SSD_GROUPS * SSD_STATE
    n_experts = w_router.shape[2]
    n = batch * seq
    fox_lane0 = n_ssd_heads
    assert n_ssd_heads + n_fox_heads <= LANES and cdim % d_ssd == 0 and d_ssd == d and d_fox == d

    tm = min(1024, n)
    o_z, o_xbc, o_dt = 0, d_ssd, d_ssd + cdim
    o_q = o_dt + n_ssd_heads
    o_k, o_v, o_f = o_q + d_fox, o_q + 2 * d_fox, o_q + 3 * d_fox
    q_col = (cdim + d_ssd) // FOX_HEAD_DIM
    k_col = q_col + n_fox_heads
    v_col = k_col + n_fox_heads

    head_of_ch = jnp.arange(d_ssd, dtype=I32) // SSD_HEAD_DIM
    e01 = (jnp.arange(LANES, dtype=I32)[:, None] == head_of_ch[None, :]).astype(BF16)

    n_assign = n * TOP_K
    nb = -(-n_assign // MOE_TM) + n_experts
    nb_pad = -(-nb // 8) * 8
    n_slots = nb * MOE_TM

    h = x.reshape(n, d)
    mem2 = mem.reshape(batch * mem_len, d)
    for i in range(depth):
        w = w_in[i]
        w_main = jnp.concatenate([w[:, o_xbc:o_dt], w[:, o_z:o_xbc], w[:, o_q:o_f]], axis=1).astype(BF16)
        w_small = jnp.concatenate(
            [w[:, o_dt:o_q], w[:, o_f:], jnp.zeros((d, LANES - n_ssd_heads - n_fox_heads), F32)], axis=1).astype(BF16)
        pmain, psmall = _inproj(h, ln_mix[i][None, :], w_main, w_small, tm, 1024)

        y_ssd, fcol, frow = _ssd(
            pmain, psmall, conv_w[i], conv_b[i][None, :], _pad_lanes(dt_bias[i]), _pad_lanes(a_log[i]),
            _pad_lanes(fgate_bias[i], fox_lane0), jnp.repeat(d_skip[i].astype(F32), SSD_HEAD_DIM)[None, :],
            ssd_norm[i][None, :], e01, batch=batch, seq=seq, d_inner=d_ssd, n_heads=n_ssd_heads, fox_lane0=fox_lane0)
        y_fox = _fox(pmain, fcol, frow, batch=batch, seq=seq, n_heads=n_fox_heads, q_col=q_col, k_col=k_col,
                     v_col=v_col, fox_lane0=fox_lane0, tq=min(256, seq))
        h = _mm_res([y_ssd, y_fox], w_out[i].astype(BF16), h, tm, 512, "out_proj")

        q = _norm_matmul(h, ln_xa[i][None, :], w_xq[i].astype(BF16), tm, 1024, "xa_q")
        kv = _norm_matmul(mem2, ln_mem[i][None, :], w_xkv[i].astype(BF16), min(1024, batch * mem_len), 1024, "xa_kv")
        o = _xattn(q, kv, batch=batch, seq=seq, mem_len=mem_len, d_model=d, tq=min(512, seq))
        h = _mm_res([o], w_xo[i].astype(BF16), h, tm, 512, "xa_o")

        wr = jnp.concatenate([w_router[i], jnp.zeros((d, LANES - n_experts), F32)], axis=1)
        idx, gate, cnt = _router(h, ln_ffn[i][None, :], wr, _pad_lanes(b_router[i]), n_experts=n_experts,
                                 tm=min(512, n))
        dest, be, meta = _slots(idx, cnt, n_experts=n_experts, tm_e=MOE_TM, nb_pad=nb_pad, t=min(512, n))
        dest_flat = dest[:, :TOP_K].reshape(-1)
        fill_lo = meta[0, :n_experts] + meta[1, :n_experts]
        fill_hi = meta[2, :n_experts]
        buf = _dispatch(dest_flat, fill_lo, fill_hi, h, ln_ffn[i][None, :], n_slots=n_slots, n_experts=n_experts,
                        t=min(MOE_ROW_TILE, n))
        ybuf = _experts(be[:nb, 0], meta[3, :1], buf, w_moe1[i], b_moe1[i][:, None, :], w_moe2[i],
                        b_moe2[i][:, None, :], tm=MOE_TM, tc=MOE_TC)
        last = i == depth - 1
        assert last, "the fused final norm assumes a single layer"
        h = _combine(dest_flat, gate, h, ln_final[None, :], ybuf, t=min(MOE_ROW_TILE, n))
    return h.reshape(batch, seq, d)
```
